```python
import jax, jax.numpy as jnp
from jax import lax
import numpy as np

D_MODEL = 2048
BATCH = 4
SEQ = 4096
DEPTH = 1

SB_HEADS = 8
SB_HEAD_DIM = 128
SB_WIDTH = SB_HEADS * SB_HEAD_DIM
MLA_HEADS = 8
MLA_NOPE_DIM = 128
MLA_ROPE_DIM = 64
MLA_V_DIM = 128
MLA_QK_DIM = MLA_NOPE_DIM + MLA_ROPE_DIM
Q_LORA_RANK = 512
KV_LORA_RANK = 512
MLA_WIDTH = MLA_HEADS * MLA_V_DIM
ROPE_THETA = 10000.0
MIX_WIDTH = SB_WIDTH + MLA_WIDTH
IN_SPLITS = [SB_WIDTH, 2 * SB_WIDTH, 3 * SB_WIDTH,
             3 * SB_WIDTH + Q_LORA_RANK, 3 * SB_WIDTH + Q_LORA_RANK + KV_LORA_RANK]
IN_WIDTH = 3 * SB_WIDTH + Q_LORA_RANK + KV_LORA_RANK + MLA_ROPE_DIM
Q_BLOCK = 128
N_EXPERTS = 32
TOP_K = 4
D_EXPERT = D_MODEL
SWIGLU_ALPHA = 1.702
SWIGLU_LIMIT = 7.0
ROW_BLOCK = 128
NORM_EPS = 1e-6
N_MOD = 6

kernel_name = "hybrid_sb_mla_moe_adaln_layer"


def rmsnorm(x, gain):
    x32 = x.astype(jnp.float32)
    y = x32 * lax.rsqrt(jnp.mean(x32 * x32, axis=-1, keepdims=True) + NORM_EPS)
    return (y * gain.astype(jnp.float32)).astype(x.dtype)


def modulate(h, shift, scale):
    return h * (1.0 + scale[:, None, :]) + shift[:, None, :]


def rope_tables(positions, rot_dim):
    half = rot_dim // 2
    inv_freq = 1.0 / (ROPE_THETA ** (jnp.arange(half, dtype=jnp.float32) * 2.0 / rot_dim))
    ang = positions.astype(jnp.float32)[..., None] * inv_freq
    return jnp.cos(ang)[:, :, None, :], jnp.sin(ang)[:, :, None, :]


def apply_rope(x, cos, sin):
    x32 = x.astype(jnp.float32)
    x1, x2 = jnp.split(x32, 2, axis=-1)
    return jnp.concatenate([x1 * cos - x2 * sin, x2 * cos + x1 * sin], axis=-1).astype(x.dtype)


def to_blocks(t):
    b, s = t.shape[:2]
    return t.reshape((b, s // Q_BLOCK, Q_BLOCK) + t.shape[2:]).swapaxes(0, 1)


def from_blocks(t):
    nb, b = t.shape[:2]
    return t.swapaxes(0, 1).reshape((b, nb * Q_BLOCK) + t.shape[3:])


def stick_breaking_attention(q, k, v):
    s_len, dh = q.shape[1], q.shape[-1]
    k32 = k.astype(jnp.float32)
    key_idx = jnp.arange(s_len)
    n_blocks = s_len // Q_BLOCK

    def block(args):
        q_blk, i = args
        z = jnp.einsum("bqhd,bkhd->bhqk", q_blk.astype(jnp.float32), k32) * (dh ** -0.5)
        q_idx = i * Q_BLOCK + jnp.arange(Q_BLOCK)
        mask = key_idx[None, :] < q_idx[:, None]
        log_keep = jnp.where(mask, jax.nn.log_sigmoid(-z), 0.0)
        shifted = jnp.concatenate([log_keep[..., 1:], jnp.zeros_like(log_keep[..., :1])], axis=-1)
        later = lax.cumsum(shifted, axis=3, reverse=True)
        a = jnp.where(mask, jnp.exp(jax.nn.log_sigmoid(z) + later), 0.0)
        return jnp.einsum("bhqk,bkhd->bqhd", a.astype(v.dtype), v)

    out = lax.map(block, (to_blocks(q), jnp.arange(n_blocks)))
    return from_blocks(out)


def mla_attention(q_nope, q_rope, k_nope, k_rope, v):
    s_len = q_nope.shape[1]
    kn32 = k_nope.astype(jnp.float32)
    kr32 = k_rope.astype(jnp.float32)
    key_idx = jnp.arange(s_len)
    n_blocks = s_len // Q_BLOCK
    scale = MLA_QK_DIM ** -0.5

    def block(args):
        qn, qr, i = args
        sc = (jnp.einsum("bqhd,bkhd->bhqk", qn.astype(jnp.float32), kn32)
              + jnp.einsum("bqhr,bkr->bhqk", qr.astype(jnp.float32), kr32)) * scale
        q_idx = i * Q_BLOCK + jnp.arange(Q_BLOCK)
        mask = key_idx[None, :] <= q_idx[:, None]
        p = jax.nn.softmax(jnp.where(mask, sc, -jnp.inf), axis=-1)
        return jnp.einsum("bhqk,bkhd->bqhd", p.astype(v.dtype), v)

    out = lax.map(block, (to_blocks(q_nope), to_blocks(q_rope), jnp.arange(n_blocks)))
    return from_blocks(out)


def clamped_swiglu(u):
    glu = jnp.minimum(u[..., ::2], SWIGLU_LIMIT)
    lin = jnp.clip(u[..., 1::2], -SWIGLU_LIMIT, SWIGLU_LIMIT)
    return glu * jax.nn.sigmoid(SWIGLU_ALPHA * glu) * (lin + 1.0)


def moe_ffn(h, w_router, b_router, w1, b1, w2, b2):
    b, s, d = h.shape
    n = b * s
    hf = h.reshape(n, d)
    logits = (hf @ w_router + b_router).astype(jnp.float32)
    top_val, top_idx = lax.top_k(logits, TOP_K)
    top_w = jax.nn.softmax(top_val, axis=-1)
    n_assign = n * TOP_K
    e_flat = top_idx.reshape(-1)
    w_flat = top_w.reshape(-1)
    tok_flat = jnp.arange(n_assign, dtype=jnp.int32) // TOP_K
    order = jnp.argsort(e_flat)
    e_sorted = e_flat[order]
    counts = jnp.bincount(e_flat, length=N_EXPERTS)
    starts = jnp.cumsum(counts) - counts
    padded = (counts + ROW_BLOCK - 1) // ROW_BLOCK * ROW_BLOCK
    padded_ends = jnp.cumsum(padded)
    padded_starts = padded_ends - padded
    dest = padded_starts[e_sorted] + jnp.arange(n_assign, dtype=jnp.int32) - starts[e_sorted]
    n_rows = n_assign + N_EXPERTS * ROW_BLOCK
    n_row_blocks = n_rows // ROW_BLOCK
    row_tok = jnp.zeros((n_rows,), jnp.int32).at[dest].set(tok_flat[order])
    row_w = jnp.zeros((n_rows,), jnp.float32).at[dest].set(w_flat[order])
    block_start = jnp.arange(n_row_blocks, dtype=jnp.int32) * ROW_BLOCK
    block_e = jnp.minimum(jnp.searchsorted(padded_ends, block_start, side="right"), N_EXPERTS - 1)
    xs = hf[row_tok].reshape(n_row_blocks, ROW_BLOCK, d)

    def expert_block(args):
        x_blk, e = args
        u = x_blk @ w1[e] + b1[e]
        return clamped_swiglu(u) @ w2[e] + b2[e]

    ys = lax.map(expert_block, (xs, block_e)).reshape(n_rows, d)
    out = jnp.zeros((n, d), jnp.float32).at[row_tok].add(row_w[:, None] * ys.astype(jnp.float32))
    return out.astype(h.dtype).reshape(b, s, d)


def setup_inputs(seed: int = 0) -> dict:
    key = jax.random.key(seed)
    ks = jax.random.split(key, 24)
    f32 = jnp.float32

    def nrm(k, shape, fan_in):
        return jax.random.normal(k, shape, f32) * (fan_in ** -0.5)

    def gain(k, shape):
        return 1.0 + 0.05 * jax.random.normal(k, shape, f32)

    def small(k, shape, s=0.02):
        return s * jax.random.normal(k, shape, f32)

    offsets = jax.random.randint(ks[2], (BATCH, 1), 0, 1024, dtype=jnp.int32)
    positions = offsets + jnp.arange(SEQ, dtype=jnp.int32)[None, :]
    return {
        "x": jax.random.normal(ks[0], (BATCH, SEQ, D_MODEL), f32),
        "c": jax.random.normal(ks[1], (BATCH, D_MODEL), f32),
        "positions": positions,
        "g_attn": gain(ks[3], (DEPTH, D_MODEL)),
        "w_mod": nrm(ks[4], (DEPTH, D_MODEL, N_MOD * D_MODEL), D_MODEL),
        "b_mod": small(ks[5], (DEPTH, N_MOD * D_MODEL)),
        "w_in": nrm(ks[6], (DEPTH, D_MODEL, IN_WIDTH), D_MODEL),
        "g_q_lat": gain(ks[7], (DEPTH, Q_LORA_RANK)),
        "w_q_up": nrm(ks[8], (DEPTH, Q_LORA_RANK, MLA_HEADS * MLA_QK_DIM), Q_LORA_RANK),
        "g_kv_lat": gain(ks[9], (DEPTH, KV_LORA_RANK)),
        "w_kv_up": nrm(ks[10], (DEPTH, KV_LORA_RANK, MLA_HEADS * (MLA_NOPE_DIM + MLA_V_DIM)), KV_LORA_RANK),
        "w_out": nrm(ks[11], (DEPTH, MIX_WIDTH, D_MODEL), MIX_WIDTH),
        "g_ffn": gain(ks[12], (DEPTH, D_MODEL)),
        "w_router": nrm(ks[13], (DEPTH, D_MODEL, N_EXPERTS), D_MODEL),
        "b_router": small(ks[14], (DEPTH, N_EXPERTS), 0.01),
        "w1": nrm(ks[15], (DEPTH, N_EXPERTS, D_MODEL, 2 * D_EXPERT), D_MODEL),
        "b1": small(ks[16], (DEPTH, N_EXPERTS, 2 * D_EXPERT)),
        "w2": nrm(ks[17], (DEPTH, N_EXPERTS, D_EXPERT, D_MODEL), D_EXPERT),
        "b2": small(ks[18], (DEPTH, N_EXPERTS, D_MODEL)),
        "g_final": gain(ks[19], (D_MODEL,)),
    }


def reference(x, c, positions, g_attn, w_mod, b_mod, w_in, g_q_lat, w_q_up, g_kv_lat, w_kv_up,
              w_out, g_ffn, w_router, b_router, w1, b1, w2, b2, g_final):
    b, s, _ = x.shape
    cos, sin = rope_tables(positions, MLA_ROPE_DIM)
    c_act = jax.nn.silu(c)
    for l in range(DEPTH):
        mod = c_act @ w_mod[l] + b_mod[l]
        shift1, scale1, gate1, shift2, scale2, gate2 = jnp.split(mod, N_MOD, axis=-1)

        h = modulate(rmsnorm(x, g_attn[l]), shift1, scale1)
        proj = h @ w_in[l]
        sb_q, sb_k, sb_v, q_lat, kv_lat, k_rope = jnp.split(proj, IN_SPLITS, axis=-1)
        sb_out = stick_breaking_attention(
            sb_q.reshape(b, s, SB_HEADS, SB_HEAD_DIM),
            sb_k.reshape(b, s, SB_HEADS, SB_HEAD_DIM),
            sb_v.reshape(b, s, SB_HEADS, SB_HEAD_DIM))
        q = (rmsnorm(q_lat, g_q_lat[l]) @ w_q_up[l]).reshape(b, s, MLA_HEADS, MLA_QK_DIM)
        q_nope, q_rope = q[..., :MLA_NOPE_DIM], q[..., MLA_NOPE_DIM:]
        kv = (rmsnorm(kv_lat, g_kv_lat[l]) @ w_kv_up[l]).reshape(b, s, MLA_HEADS, MLA_NOPE_DIM + MLA_V_DIM)
        k_nope, v = kv[..., :MLA_NOPE_DIM], kv[..., MLA_NOPE_DIM:]
        q_rope = apply_rope(q_rope, cos, sin)
        k_rope = apply_rope(k_rope[:, :, None, :], cos, sin)[:, :, 0, :]
        mla_out = mla_attention(q_nope, q_rope, k_nope, k_rope, v)
        mixed = jnp.concatenate([sb_out.reshape(b, s, SB_WIDTH), mla_out.reshape(b, s, MLA_WIDTH)], axis=-1)
        x = x + gate1[:, None, :] * (mixed @ w_out[l])

        h2 = modulate(rmsnorm(x, g_ffn[l]), shift2, scale2)
        x = x + gate2[:, None, :] * moe_ffn(h2, w_router[l], b_router[l], w1[l], b1[l], w2[l], b2[l])
    return rmsnorm(x, g_final)
```

```python
import functools

import jax
import jax.numpy as jnp
from jax import lax
from jax.experimental import pallas as pl
from jax.experimental.pallas import tpu as pltpu

F32 = jnp.float32
BF16 = jnp.bfloat16
U32 = jnp.uint32
I32 = jnp.int32

SB_HEADS = 8
MLA_HEADS = 8
HEAD_DIM = 128
ROPE_DIM = 64
ROPE_HALF = ROPE_DIM // 2
MLA_QK_DIM = HEAD_DIM + ROPE_DIM
LORA = 512
ROPE_THETA = 10000.0
N_EXPERTS = 32
TOP_K = 4
SWIGLU_ALPHA = 1.702
SWIGLU_LIMIT = 7.0
NORM_EPS = 1e-6
SB_WIDTH = SB_HEADS * HEAD_DIM
MLA_WIDTH = MLA_HEADS * HEAD_DIM

LANES = 128
VMEM_LIMIT_BYTES = 56 * 1024 * 1024

ROW_TILE = 256
ATTN_TILE = 256
MOE_TILE = 1024
MOE_SUB = 256
MOE_FC = 256
COMBINE_TILE = 128
NEG_BIG = -1e30


def _cparams(sem):
    return pltpu.CompilerParams(dimension_semantics=sem, vmem_limit_bytes=VMEM_LIMIT_BYTES)


def _resident(shape):
    nd = len(shape)
    return pl.BlockSpec(shape, lambda *_: (0,) * nd, pipeline_mode=pl.Buffered(1))


def _rms(v):
    return v * lax.rsqrt(jnp.mean(v * v, axis=-1, keepdims=True) + NORM_EPS)


def _mod_kernel(c_ref, w_ref, b_ref, o_ref):
    c = c_ref[...]
    a = (c * jax.nn.sigmoid(c)).astype(BF16)
    o_ref[...] = jnp.dot(a, w_ref[...].astype(BF16), preferred_element_type=F32) + b_ref[...]


def _mod(c_pad, w_mod, b_mod):
    rows, d = c_pad.shape
    n_out = w_mod.shape[1]
    tn = 1024
    return pl.pallas_call(
        _mod_kernel,
        out_shape=jax.ShapeDtypeStruct((rows, n_out), F32),
        grid=(n_out // tn,),
        in_specs=[
            pl.BlockSpec((rows, d), lambda j: (0, 0)),
            pl.BlockSpec((d, tn), lambda j: (0, j)),
            pl.BlockSpec((1, tn), lambda j: (0, j)),
        ],
        out_specs=pl.BlockSpec((rows, tn), lambda j: (0, j)),
        compiler_params=_cparams(("arbitrary",)),
        name="mod",
    )(c_pad, w_mod, b_mod)


def _inproj_kernel(x_ref, sc_ref, sh_ref, g_ref, pos_ref, invf_ref, cm_ref, sm_ref, w_ref,
                   gq_ref, wq_ref, gkv_ref, wkv_ref,
                   sbq_ref, sbk_ref, sbv_ref, mq_ref, mkv_ref, mkr_ref):
    h = (_rms(x_ref[...]) * g_ref[...] * sc_ref[0] + sh_ref[0]).astype(BF16)
    proj = jnp.dot(h, w_ref[...], preferred_element_type=F32)

    sbq_ref[...] = (proj[:, :SB_WIDTH] * (HEAD_DIM ** -0.5)).astype(BF16)
    sbk_ref[...] = proj[:, SB_WIDTH:2 * SB_WIDTH].astype(BF16)
    sbv_ref[...] = proj[:, 2 * SB_WIDTH:3 * SB_WIDTH].astype(BF16)

    ang = pos_ref[...].astype(F32) * invf_ref[...]
    cosm = jnp.cos(ang) * cm_ref[...]
    sinm = jnp.sin(ang) * sm_ref[...]

    def rope(v):
        return v * cosm + pltpu.roll(v, LANES // 2, 1) * sinm

    o = 3 * SB_WIDTH
    q_lat = (_rms(proj[:, o:o + LORA]) * gq_ref[...]).astype(BF16)
    q = jnp.dot(q_lat, wq_ref[...], preferred_element_type=F32) * (MLA_QK_DIM ** -0.5)
    for hd in range(MLA_HEADS):
        c0 = hd * 2 * HEAD_DIM
        mq_ref[:, c0:c0 + HEAD_DIM] = q[:, c0:c0 + HEAD_DIM].astype(BF16)
        mq_ref[:, c0 + HEAD_DIM:c0 + 2 * HEAD_DIM] = rope(q[:, c0 + HEAD_DIM:c0 + 2 * HEAD_DIM]).astype(BF16)

    kv_lat = (_rms(proj[:, o + LORA:o + 2 * LORA]) * gkv_ref[...]).astype(BF16)
    mkv_ref[...] = jnp.dot(kv_lat, wkv_ref[...], preferred_element_type=F32).astype(BF16)
    mkr_ref[...] = rope(proj[:, o + 2 * LORA:o + 2 * LORA + LANES]).astype(BF16)


def _inproj(x2, sc1, sh1, g_attn, pos, invf, cmask, smask, w_all, g_q, wq, g_kv, wkv, seq):
    n, d = x2.shape
    tm = ROW_TILE
    wcols = w_all.shape[1]
    per_b = seq // tm
    row = lambda i: (i, 0)
    bat = lambda i: (i // per_b, 0, 0)
    outs = [
        jax.ShapeDtypeStruct((n, SB_WIDTH), BF16),
        jax.ShapeDtypeStruct((n, SB_WIDTH), BF16),
        jax.ShapeDtypeStruct((n, SB_WIDTH), BF16),
        jax.ShapeDtypeStruct((n, 2 * MLA_WIDTH), BF16),
        jax.ShapeDtypeStruct((n, 2 * MLA_WIDTH), BF16),
        jax.ShapeDtypeStruct((n, LANES), BF16),
    ]
    return pl.pallas_call(
        _inproj_kernel,
        out_shape=outs,
        grid=(n // tm,),
        in_specs=[
            pl.BlockSpec((tm, d), row),
            pl.BlockSpec((1, 1, d), bat),
            pl.BlockSpec((1, 1, d), bat),
            _resident((1, d)),
            pl.BlockSpec((tm, 1), row),
            _resident((1, LANES)),
            _resident((1, LANES)),
            _resident((1, LANES)),
            _resident((d, wcols)),
            _resident((1, LORA)),
            _resident((LORA, 2 * MLA_WIDTH)),
            _resident((1, LORA)),
            _resident((LORA, 2 * MLA_WIDTH)),
        ],
        out_specs=[
            pl.BlockSpec((tm, SB_WIDTH), row),
            pl.BlockSpec((tm, SB_WIDTH), row),
            pl.BlockSpec((tm, SB_WIDTH), row),
            pl.BlockSpec((tm, 2 * MLA_WIDTH), row),
            pl.BlockSpec((tm, 2 * MLA_WIDTH), row),
            pl.BlockSpec((tm, LANES), row),
        ],
        compiler_params=_cparams(("arbitrary",)),
        name="inproj",
    )(x2, sc1, sh1, g_attn, pos, invf, cmask, smask, w_all, g_q, wq, g_kv, wkv)


_NT = (((1,), (1,)), ((), ()))


def _sb_kernel(q_ref, k_ref, v_ref, o_ref, *, tile):
    i = pl.program_id(2)
    q = q_ref[...]
    row = lax.broadcasted_iota(I32, (tile, tile), 0)
    col = lax.broadcasted_iota(I32, (tile, tile), 1)
    suffix = jnp.where(row >= col, 1.0, 0.0).astype(BF16)
    strict = col < row

    def step(kt, carry, diag):
        c, acc = carry
        k0 = pl.multiple_of(kt * tile, tile)
        k = k_ref[pl.ds(k0, tile), :]
        v = v_ref[pl.ds(k0, tile), :]
        z = lax.dot_general(q, k, _NT, preferred_element_type=F32)
        l1p = jnp.log(1.0 + jnp.exp(-jnp.abs(z)))
        log_beta = jnp.minimum(z, 0.0) - l1p
        log_keep = -jnp.maximum(z, 0.0) - l1p
        if diag:
            log_keep = jnp.where(strict, log_keep, 0.0)
        hi = log_keep.astype(BF16)
        lo = (log_keep - hi.astype(F32)).astype(BF16)
        incl = (jnp.dot(hi, suffix, preferred_element_type=F32)
                + jnp.dot(lo, suffix, preferred_element_type=F32))
        a = jnp.exp(log_beta + (incl - log_keep) + c)
        if diag:
            a = jnp.where(strict, a, 0.0)
        acc = acc + jnp.dot(a.astype(BF16), v, preferred_element_type=F32)
        return c + incl[:, 0:1], acc

    carry = (jnp.zeros((tile, 1), F32), jnp.zeros((tile, HEAD_DIM), F32))
    carry = step(i, carry, True)
    carry = lax.fori_loop(0, i, lambda s, cr: step(i - 1 - s, cr, False), carry)
    o_ref[...] = carry[1].astype(BF16)


def _sb_attention(sbq, sbk, sbv, batch, seq):
    n = sbq.shape[0]
    t = ATTN_TILE
    nq = seq // t
    qmap = lambda b, h, i: (b * nq + i, h)
    kmap = lambda b, h, i: (b, h)
    return pl.pallas_call(
        functools.partial(_sb_kernel, tile=t),
        out_shape=jax.ShapeDtypeStruct((n, SB_WIDTH), BF16),
        grid=(batch, SB_HEADS, nq),
        in_specs=[
            pl.BlockSpec((t, HEAD_DIM), qmap),
            pl.BlockSpec((seq, HEAD_DIM), kmap),
            pl.BlockSpec((seq, HEAD_DIM), kmap),
        ],
        out_specs=pl.BlockSpec((t, HEAD_DIM), qmap),
        compiler_params=_cparams(("arbitrary", "arbitrary", "arbitrary")),
        name="sb_attn",
    )(sbq, sbk, sbv)


def _mla_kernel(q_ref, kv_ref, kr_ref, o_ref, kcat_ref, *, tile):
    i = pl.program_id(2)

    @pl.when(i == 0)
    def _():
        kcat_ref[:, :HEAD_DIM] = kv_ref[:, :HEAD_DIM]
        kcat_ref[:, HEAD_DIM:] = kr_ref[...]

    q = q_ref[...]
    row = lax.broadcasted_iota(I32, (tile, tile), 0)
    col = lax.broadcasted_iota(I32, (tile, tile), 1)
    causal = col <= row

    def step(kt, carry, diag):
        m, l, acc = carry
        k0 = pl.multiple_of(kt * tile, tile)
        k = kcat_ref[pl.ds(k0, tile), :]
        v = kv_ref[pl.ds(k0, tile), HEAD_DIM:]
        s = lax.dot_general(q, k, _NT, preferred_element_type=F32)
        if diag:
            s = jnp.where(causal, s, NEG_BIG)
        m_new = jnp.maximum(m, jnp.max(s, axis=1, keepdims=True))
        p = jnp.exp(s - m_new)
        alpha = jnp.exp(m - m_new)
        l = alpha * l + jnp.sum(p, axis=1, keepdims=True)
        acc = alpha * acc + jnp.dot(p.astype(BF16), v, preferred_element_type=F32)
        return m_new, l, acc

    carry = (jnp.full((tile, 1), NEG_BIG, F32), jnp.zeros((tile, 1), F32),
             jnp.zeros((tile, HEAD_DIM), F32))
    carry = step(i, carry, True)
    carry = lax.fori_loop(0, i, lambda s, cr: step(s, cr, False), carry)
    o_ref[...] = (carry[2] / carry[1]).astype(BF16)


def _mla_attention(mq, mkv, mkr, batch, seq):
    n = mq.shape[0]
    t = ATTN_TILE
    nq = seq // t
    return pl.pallas_call(
        functools.partial(_mla_kernel, tile=t),
        out_shape=jax.ShapeDtypeStruct((n, MLA_WIDTH), BF16),
        grid=(batch, MLA_HEADS, nq),
        in_specs=[
            pl.BlockSpec((t, 2 * HEAD_DIM), lambda b, h, i: (b * nq + i, h)),
            pl.BlockSpec((seq, 2 * HEAD_DIM), lambda b, h, i: (b, h)),
            pl.BlockSpec((seq, LANES), lambda b, h, i: (b, 0)),
        ],
        out_specs=pl.BlockSpec((t, HEAD_DIM), lambda b, h, i: (b * nq + i, h)),
        scratch_shapes=[pltpu.VMEM((seq, 2 * HEAD_DIM), BF16)],
        compiler_params=_cparams(("arbitrary", "arbitrary", "arbitrary")),
        name="mla_attn",
    )(mq, mkv, mkr)


def _pack_bf16_pairs(lo_half, hi_half):
    lo = pltpu.bitcast(lo_half, U32) >> 16
    hi = pltpu.bitcast(hi_half, U32) & jnp.uint32(0xFFFF0000)
    return lo | hi


def _outproj_kernel(sb_ref, mla_ref, x_ref, g1_ref, sc_ref, sh_ref, g_ref, wo_ref, wr_ref, br_ref,
                    x1_ref, hp_ref, ii_ref, iw_ref, cnt_ref, run_ref):
    step = pl.program_id(0)
    tm, d = x_ref.shape

    @pl.when(step == 0)
    def _():
        run_ref[...] = jnp.zeros_like(run_ref)

    o = (jnp.dot(sb_ref[...], wo_ref[:SB_WIDTH, :], preferred_element_type=F32)
         + jnp.dot(mla_ref[...], wo_ref[SB_WIDTH:, :], preferred_element_type=F32))
    x1 = x_ref[...] + g1_ref[0] * o
    x1_ref[...] = x1
    h2 = _rms(x1) * g_ref[...] * sc_ref[0] + sh_ref[0]
    hb = h2.astype(BF16)
    hb32 = hb.astype(F32)
    hp_ref[...] = _pack_bf16_pairs(hb32[:, :d // 2], hb32[:, d // 2:])

    h_lo = (h2 - hb32).astype(BF16)
    w = wr_ref[...]
    w_hi = w.astype(BF16)
    w_lo = (w - w_hi.astype(F32)).astype(BF16)
    logits = (jnp.dot(hb, w_hi, preferred_element_type=F32)
              + jnp.dot(h_lo, w_hi, preferred_element_type=F32)
              + jnp.dot(hb, w_lo, preferred_element_type=F32)) + br_ref[...]

    lane = lax.broadcasted_iota(I32, (tm, N_EXPERTS), 1)
    work = logits
    vals, idxs, hots = [], [], []
    for _ in range(TOP_K):
        m = jnp.max(work, axis=1, keepdims=True)
        idx = jnp.min(jnp.where(work == m, lane, N_EXPERTS), axis=1, keepdims=True)
        hot = lane == idx
        work = jnp.where(hot, -jnp.inf, work)
        vals.append(m)
        idxs.append(idx)
        hots.append(hot)
    exps = [jnp.exp(v - vals[0]) for v in vals]
    den = exps[0] + exps[1] + exps[2] + exps[3]
    sel = jnp.where(hots[0] | hots[1] | hots[2] | hots[3], 1.0, 0.0)

    r = lax.broadcasted_iota(I32, (tm, tm), 0)
    c = lax.broadcasted_iota(I32, (tm, tm), 1)
    before = jnp.where(c < r, 1.0, 0.0).astype(BF16)
    rank = jnp.dot(before, sel.astype(BF16), preferred_element_type=F32) + run_ref[...]
    run_ref[...] = run_ref[...] + jnp.sum(sel, axis=0, keepdims=True)
    cnt_ref[...] = run_ref[...]

    lane128 = lax.broadcasted_iota(I32, (tm, LANES), 1)
    info_i = jnp.zeros((tm, LANES), I32)
    info_w = jnp.zeros((tm, LANES), F32)
    for k in range(TOP_K):
        rank_k = jnp.sum(jnp.where(hots[k], rank, 0.0), axis=1, keepdims=True).astype(I32)
        info_i = jnp.where(lane128 == k, idxs[k], info_i)
        info_i = jnp.where(lane128 == TOP_K + k, rank_k, info_i)
        info_w = jnp.where(lane128 == k, exps[k] / den, info_w)
    ii_ref[...] = info_i
    iw_ref[...] = info_w


def _outproj(sb, mla, x2, g1, sc2, sh2, g_ffn, wo, w_router, b_router, seq):
    n, d = x2.shape
    tm = ROW_TILE
    per_b = seq // tm
    row = lambda i: (i, 0)
    bat = lambda i: (i // per_b, 0, 0)
    outs = [
        jax.ShapeDtypeStruct((n, d), F32),
        jax.ShapeDtypeStruct((n, d // 2), U32),
        jax.ShapeDtypeStruct((n, LANES), I32),
        jax.ShapeDtypeStruct((n, LANES), F32),
        jax.ShapeDtypeStruct((1, N_EXPERTS), F32),
    ]
    return pl.pallas_call(
        _outproj_kernel,
        out_shape=outs,
        grid=(n // tm,),
        in_specs=[
            pl.BlockSpec((tm, SB_WIDTH), row),
            pl.BlockSpec((tm, MLA_WIDTH), row),
            pl.BlockSpec((tm, d), row),
            pl.BlockSpec((1, 1, d), bat),
            pl.BlockSpec((1, 1, d), bat),
            pl.BlockSpec((1, 1, d), bat),
            _resident((1, d)),
            _resident((SB_WIDTH + MLA_WIDTH, d)),
            _resident((d, N_EXPERTS)),
            _resident((1, N_EXPERTS)),
        ],
        out_specs=[
            pl.BlockSpec((tm, d), row),
            pl.BlockSpec((tm, d // 2), row),
            pl.BlockSpec((tm, LANES), row),
            pl.BlockSpec((tm, LANES), row),
            pl.BlockSpec((1, N_EXPERTS), lambda i: (0, 0)),
        ],
        scratch_shapes=[pltpu.VMEM((1, N_EXPERTS), F32)],
        compiler_params=_cparams(("arbitrary",)),
        name="outproj",
    )(sb, mla, x2, g1, sc2, sh2, g_ffn, wo, w_router, b_router)


def _row_copy(src_ref, src_row, dst_ref, dst_row, sem):
    return pltpu.make_async_copy(src_ref.at[pl.ds(src_row, 1), :], dst_ref.at[pl.ds(dst_row, 1), :], sem)


def _dispatch_kernel(dest_ref, h_ref, xs_in_ref, xs_ref, sem):
    del xs_in_ref
    tm = h_ref.shape[0]

    def issue(r, carry):
        for k in range(TOP_K):
            _row_copy(h_ref, r, xs_ref, dest_ref[r * TOP_K + k], sem).start()
        return carry

    def drain(r, carry):
        for k in range(TOP_K):
            _row_copy(h_ref, 0, xs_ref, 0, sem).wait()
        return carry

    lax.fori_loop(0, tm, issue, 0)
    lax.fori_loop(0, tm, drain, 0)


def _dispatch(dest_flat, hp, xs_init):
    n, w = hp.shape
    tm = ROW_TILE
    return pl.pallas_call(
        _dispatch_kernel,
        out_shape=jax.ShapeDtypeStruct(xs_init.shape, xs_init.dtype),
        grid=(n // tm,),
        in_specs=[
            pl.BlockSpec((tm * TOP_K,), lambda i: (i,), memory_space=pltpu.SMEM),
            pl.BlockSpec((tm, w), lambda i: (i, 0)),
            pl.BlockSpec(memory_space=pl.ANY),
        ],
        out_specs=pl.BlockSpec(memory_space=pl.ANY),
        scratch_shapes=[pltpu.SemaphoreType.DMA(())],
        input_output_aliases={2: 0},
        compiler_params=_cparams(("arbitrary",)),
        name="dispatch",
    )(dest_flat, hp, xs_init)


def _moe_kernel(te_ref, tx_ref, tv_ref, ns_ref, xs_ref, w1_ref, b1g_ref, b1l_ref, w2_ref, b2_ref, p_ref,
                ys_ref, xb_ref, gl_ref, w2b_ref):
    del te_ref, tx_ref
    t = pl.program_id(0)
    j = pl.program_id(1)
    half = xs_ref.shape[1]
    fc = w2_ref.shape[0]

    @pl.when(tv_ref[t] == 1)
    def _():
        @pl.when(j == 0)
        def _():
            w = xs_ref[...]
            xb_ref[:, :half] = pltpu.bitcast(w << 16, F32).astype(BF16)
            xb_ref[:, half:] = pltpu.bitcast(w & jnp.uint32(0xFFFF0000), F32).astype(BF16)
            ys_ref[...] = jnp.broadcast_to(b2_ref[...], ys_ref.shape)

        p = p_ref[...]
        for blk in range(2 * fc // 256):
            wb = w1_ref[:, blk * 256:(blk + 1) * 256].astype(BF16)
            split = jnp.dot(wb, p, preferred_element_type=F32).astype(BF16)
            gl_ref[:, blk * LANES:(blk + 1) * LANES] = split[:, :LANES]
            gl_ref[:, fc + blk * LANES:fc + (blk + 1) * LANES] = split[:, LANES:]
        w2b_ref[...] = w2_ref[...].astype(BF16)

        def sub(s, carry):
            r0 = pl.multiple_of(s * MOE_SUB, MOE_SUB)
            xt = xb_ref[pl.ds(r0, MOE_SUB), :]
            ug = jnp.dot(xt, gl_ref[:, :fc], preferred_element_type=F32) + b1g_ref[...]
            ul = jnp.dot(xt, gl_ref[:, fc:], preferred_element_type=F32) + b1l_ref[...]
            glu = jnp.minimum(ug, SWIGLU_LIMIT)
            lin = jnp.clip(ul, -SWIGLU_LIMIT, SWIGLU_LIMIT)
            act = glu * jax.nn.sigmoid(SWIGLU_ALPHA * glu) * (lin + 1.0)
            ys_ref[pl.ds(r0, MOE_SUB), :] += jnp.dot(act.astype(BF16), w2b_ref[...],
                                                     preferred_element_type=F32)
            return carry

        lax.fori_loop(0, ns_ref[t], sub, 0)


def _moe(tile_e, tile_x, tile_v, tile_ns, xs, w1, b1g, b1l, w2, b2, perm):
    rows, half = xs.shape
    d = 2 * half
    n_exp, _, f2 = w1.shape
    f = f2 // 2
    fc = MOE_FC
    nj = f // fc
    n_tiles = rows // MOE_TILE

    def jj(t, j, tv):
        return jnp.where(tv[t] == 1, j, nj - 1)

    grid_spec = pltpu.PrefetchScalarGridSpec(
        num_scalar_prefetch=4,
        grid=(n_tiles, nj),
        in_specs=[
            pl.BlockSpec((MOE_TILE, half), lambda t, j, te, tx, tv, ns: (tx[t], 0)),
            pl.BlockSpec((None, d, 2 * fc), lambda t, j, te, tx, tv, ns: (te[t], 0, jj(t, j, tv))),
            pl.BlockSpec((None, 1, fc), lambda t, j, te, tx, tv, ns: (te[t], 0, jj(t, j, tv))),
            pl.BlockSpec((None, 1, fc), lambda t, j, te, tx, tv, ns: (te[t], 0, jj(t, j, tv))),
            pl.BlockSpec((None, fc, d), lambda t, j, te, tx, tv, ns: (te[t], jj(t, j, tv), 0)),
            pl.BlockSpec((None, 1, d), lambda t, j, te, tx, tv, ns: (te[t], 0, 0)),
            pl.BlockSpec((256, 256), lambda t, j, te, tx, tv, ns: (0, 0)),
        ],
        out_specs=pl.BlockSpec((MOE_TILE, d), lambda t, j, te, tx, tv, ns: (tx[t], 0)),
        scratch_shapes=[
            pltpu.VMEM((MOE_TILE, d), BF16),
            pltpu.VMEM((d, 2 * fc), BF16),
            pltpu.VMEM((fc, d), BF16),
        ],
    )
    return pl.pallas_call(
        _moe_kernel,
        out_shape=jax.ShapeDtypeStruct((rows, d), F32),
        grid_spec=grid_spec,
        compiler_params=_cparams(("arbitrary", "arbitrary")),
        name="moe",
    )(tile_e, tile_x, tile_v, tile_ns, xs, w1, b1g, b1l, w2, b2, perm)


def _combine_kernel(dest_ref, iw_ref, x1_ref, g2_ref, gf_ref, ys_ref, o_ref, buf_ref, sem):
    tm = x1_ref.shape[0]

    def issue(r, carry):
        for k in range(TOP_K):
            _row_copy(ys_ref, dest_ref[r * TOP_K + k], buf_ref.at[k], r, sem).start()
        return carry

    def drain(r, carry):
        for k in range(TOP_K):
            _row_copy(ys_ref, 0, buf_ref.at[k], 0, sem).wait()
        return carry

    lax.fori_loop(0, tm, issue, 0)
    lax.fori_loop(0, tm, drain, 0)

    w = iw_ref[...]
    moe = w[:, 0:1] * buf_ref[0]
    for k in range(1, TOP_K):
        moe = moe + w[:, k:k + 1] * buf_ref[k]
    x = x1_ref[...] + g2_ref[0] * moe
    o_ref[...] = _rms(x) * gf_ref[...]


def _combine(dest_flat, info_w, x1, g2, g_final, ys, seq):
    n, d = x1.shape
    tm = COMBINE_TILE
    per_b = seq // tm
    return pl.pallas_call(
        _combine_kernel,
        out_shape=jax.ShapeDtypeStruct((n, d), F32),
        grid=(n // tm,),
        in_specs=[
            pl.BlockSpec((tm * TOP_K,), lambda i: (i,), memory_space=pltpu.SMEM),
            pl.BlockSpec((tm, LANES), lambda i: (i, 0)),
            pl.BlockSpec((tm, d), lambda i: (i, 0)),
            pl.BlockSpec((1, 1, d), lambda i: (i // per_b, 0, 0)),
            _resident((1, d)),
            pl.BlockSpec(memory_space=pl.ANY),
        ],
        out_specs=pl.BlockSpec((tm, d), lambda i: (i, 0)),
        scratch_shapes=[pltpu.VMEM((TOP_K, tm, d), F32), pltpu.SemaphoreType.DMA(())],
        compiler_params=_cparams(("arbitrary",)),
        name="combine",
    )(dest_flat, info_w, x1, g2, g_final, ys)


def _rope_lane_tables():
    inv_freq = 1.0 / (ROPE_THETA ** (jnp.arange(ROPE_HALF, dtype=F32) * 2.0 / ROPE_DIM))
    z = jnp.zeros((ROPE_HALF,), F32)
    o = jnp.ones((ROPE_HALF,), F32)
    invf = jnp.concatenate([inv_freq, z, inv_freq, z])[None, :]
    cmask = jnp.concatenate([o, z, o, z])[None, :]
    smask = jnp.concatenate([-o, z, o, z])[None, :]
    return invf, cmask, smask


def _pad_rope_cols(w):
    z = jnp.zeros(w.shape[:-1] + (ROPE_HALF,), w.dtype)
    return jnp.concatenate([w[..., :ROPE_HALF], z, w[..., ROPE_HALF:], z], axis=-1)


def kernel(x, c, positions, g_attn, w_mod, b_mod, w_in, g_q_lat, w_q_up, g_kv_lat, w_kv_up, w_out, g_ffn,
           w_router, b_router, w1, b1, w2, b2, g_final):
    batch, seq, d = x.shape
    n = batch * seq
    depth = w_mod.shape[0]
    assert depth == 1 and seq % ATTN_TILE == 0 and seq % ROW_TILE == 0 and d == SB_WIDTH + MLA_WIDTH
    x2 = x.reshape(n, d)

    c_pad = jnp.zeros((8, d), F32).at[:batch].set(c)
    mod = _mod(c_pad, w_mod[0], b_mod[0][None, :])[:batch]
    shift1, scale1, gate1, shift2, scale2, gate2 = [m[:, None, :] for m in jnp.split(mod, 6, axis=-1)]

    wi = w_in[0]
    o = 3 * SB_WIDTH + 2 * LORA
    w_all = jnp.concatenate([wi[:, :o], _pad_rope_cols(wi[:, o:])], axis=1).astype(BF16)
    wq = w_q_up[0].reshape(LORA, MLA_HEADS, MLA_QK_DIM)
    wq = jnp.concatenate([wq[..., :HEAD_DIM], _pad_rope_cols(wq[..., HEAD_DIM:])], axis=-1)
    wq = wq.reshape(LORA, MLA_HEADS * 2 * HEAD_DIM).astype(BF16)
    wkv = w_kv_up[0].astype(BF16)
    invf, cmask, smask = _rope_lane_tables()

    sbq, sbk, sbv, mq, mkv, mkr = _inproj(
        x2, 1.0 + scale1, shift1, g_attn, positions.reshape(n, 1), invf, cmask, smask, w_all,
        g_q_lat, wq, g_kv_lat, wkv, seq)

    sb_out = _sb_attention(sbq, sbk, sbv, batch, seq)
    mla_out = _mla_attention(mq, mkv, mkr, batch, seq)

    x1, hp, info_i, info_w, counts = _outproj(
        sb_out, mla_out, x2, gate1, 1.0 + scale2, shift2, g_ffn, w_out[0].astype(BF16),
        w_router[0], b_router, seq)

    counts = counts[0].astype(I32)
    padded = (counts + MOE_TILE - 1) // MOE_TILE * MOE_TILE
    pend = jnp.cumsum(padded)
    pstart = pend - padded
    idx = info_i[:, :TOP_K]
    dest = (pstart[idx] + info_i[:, TOP_K:2 * TOP_K]).reshape(n * TOP_K)
    rows = n * TOP_K + N_EXPERTS * MOE_TILE
    n_tiles = rows // MOE_TILE
    tile_row0 = jnp.arange(n_tiles, dtype=I32) * MOE_TILE
    tile_v = (tile_row0 < pend[-1]).astype(I32)
    last = jnp.maximum(pend[-1] // MOE_TILE - 1, 0)
    tile_x = jnp.minimum(jnp.arange(n_tiles, dtype=I32), last)
    tile_e = jnp.minimum(jnp.searchsorted(pend, tile_x * MOE_TILE, side="right"), N_EXPERTS - 1).astype(I32)
    used = counts[tile_e] - (tile_x * MOE_TILE - pstart[tile_e])
    tile_ns = jnp.clip((used + MOE_SUB - 1) // MOE_SUB, 0, MOE_TILE // MOE_SUB) * tile_v

    xs = _dispatch(dest, hp, jnp.zeros((rows, d // 2), U32))

    src = jnp.arange(256)
    perm = (jnp.arange(256)[None, :] == (src % 2 * LANES + src // 2)[:, None]).astype(BF16)
    f = w1.shape[-1] // 2
    b1r = b1[0].reshape(N_EXPERTS, 1, f, 2)
    ys = _moe(tile_e, tile_x, tile_v, tile_ns.astype(I32), xs, w1[0], b1r[..., 0], b1r[..., 1],
              w2[0], b2[0][:, None, :], perm)

    out = _combine(dest, info_w, x1, gate2, g_final[None, :], ys, seq)
    return out.reshape(batch, seq, d)
```

```python
import functools

import jax
import jax.numpy as jnp
from jax import lax
from jax.experimental import pallas as pl
from jax.experimental.pallas import tpu as pltpu

F32 = jnp.float32
BF16 = jnp.bfloat16
U32 = jnp.uint32
I32 = jnp.int32

SB_HEADS = 8
MLA_HEADS = 8
HEAD_DIM = 128
ROPE_DIM = 64
ROPE_HALF = ROPE_DIM // 2
MLA_QK_DIM = HEAD_DIM + ROPE_DIM
LORA = 512
ROPE_THETA = 10000.0
N_EXPERTS = 32
TOP_K = 4
SWIGLU_ALPHA = 1.702
SWIGLU_LIMIT = 7.0
NORM_EPS = 1e-6
SB_WIDTH = SB_HEADS * HEAD_DIM
MLA_WIDTH = MLA_HEADS * HEAD_DIM

LANES = 128
VMEM_LIMIT_BYTES = 56 * 1024 * 1024

ROW_TILE = 256
ATTN_TILE = 256
MLA_Q_TILE = 512
ATTN_HEADS_PER_STEP = 4
MLA_KV_STRIDE = 3 * HEAD_DIM
SB_DEAD_LOG = -104.0
MOE_TILE = 512
MOE_SUB = 256
MOE_FC = 256
COMBINE_TILE = 128
NEG_BIG = -1e30


def _cparams(sem):
    return pltpu.CompilerParams(dimension_semantics=sem, vmem_limit_bytes=VMEM_LIMIT_BYTES)


def _resident(shape):
    nd = len(shape)
    return pl.BlockSpec(shape, lambda *_: (0,) * nd, pipeline_mode=pl.Buffered(1))


def _rms(v):
    return v * lax.rsqrt(jnp.mean(v * v, axis=-1, keepdims=True) + NORM_EPS)


def _mod_kernel(c_ref, w_ref, b_ref, o_ref):
    c = c_ref[...]
    a = (c * jax.nn.sigmoid(c)).astype(BF16)
    o_ref[...] = jnp.dot(a, w_ref[...].astype(BF16), preferred_element_type=F32) + b_ref[...]


def _mod(c_pad, w_mod, b_mod):
    rows, d = c_pad.shape
    n_out = w_mod.shape[1]
    tn = 1024
    return pl.pallas_call(
        _mod_kernel,
        out_shape=jax.ShapeDtypeStruct((rows, n_out), F32),
        grid=(n_out // tn,),
        in_specs=[
            pl.BlockSpec((rows, d), lambda j: (0, 0)),
            pl.BlockSpec((d, tn), lambda j: (0, j)),
            pl.BlockSpec((1, tn), lambda j: (0, j)),
        ],
        out_specs=pl.BlockSpec((rows, tn), lambda j: (0, j)),
        compiler_params=_cparams(("arbitrary",)),
        name="mod",
    )(c_pad, w_mod, b_mod)


def _inproj_kernel(x_ref, sc_ref, sh_ref, g_ref, pos_ref, invf_ref, cm_ref, sm_ref, w_ref,
                   gq_ref, wq_ref, gkv_ref, wkv_ref,
                   sbq_ref, sbk_ref, sbv_ref, mq_ref, mkv_ref):
    h = (_rms(x_ref[...]) * g_ref[...] * sc_ref[0] + sh_ref[0]).astype(BF16)
    proj = jnp.dot(h, w_ref[...], preferred_element_type=F32)

    sbq_ref[...] = (proj[:, :SB_WIDTH] * (HEAD_DIM ** -0.5)).astype(BF16)
    sbk_ref[...] = proj[:, SB_WIDTH:2 * SB_WIDTH].astype(BF16)
    sbv_ref[...] = proj[:, 2 * SB_WIDTH:3 * SB_WIDTH].astype(BF16)

    ang = pos_ref[...].astype(F32) * invf_ref[...]
    cosm = jnp.cos(ang) * cm_ref[...]
    sinm = jnp.sin(ang) * sm_ref[...]

    def rope(v):
        return v * cosm + pltpu.roll(v, LANES // 2, 1) * sinm

    o = 3 * SB_WIDTH
    q_lat = (_rms(proj[:, o:o + LORA]) * gq_ref[...]).astype(BF16)
    q = jnp.dot(q_lat, wq_ref[...], preferred_element_type=F32) * (MLA_QK_DIM ** -0.5)
    for hd in range(MLA_HEADS):
        c0 = hd * 2 * HEAD_DIM
        mq_ref[:, c0:c0 + HEAD_DIM] = q[:, c0:c0 + HEAD_DIM].astype(BF16)
        mq_ref[:, c0 + HEAD_DIM:c0 + 2 * HEAD_DIM] = rope(q[:, c0 + HEAD_DIM:c0 + 2 * HEAD_DIM]).astype(BF16)

    kv_lat = (_rms(proj[:, o + LORA:o + 2 * LORA]) * gkv_ref[...]).astype(BF16)
    kv = jnp.dot(kv_lat, wkv_ref[...], preferred_element_type=F32).astype(BF16)
    k_rope = rope(proj[:, o + 2 * LORA:o + 2 * LORA + LANES]).astype(BF16)
    for hd in range(MLA_HEADS):
        c0 = hd * MLA_KV_STRIDE
        mkv_ref[:, c0:c0 + HEAD_DIM] = kv[:, hd * 2 * HEAD_DIM:hd * 2 * HEAD_DIM + HEAD_DIM]
        mkv_ref[:, c0 + HEAD_DIM:c0 + 2 * HEAD_DIM] = k_rope
        mkv_ref[:, c0 + 2 * HEAD_DIM:c0 + 3 * HEAD_DIM] = kv[:, hd * 2 * HEAD_DIM + HEAD_DIM:(hd + 1) * 2 * HEAD_DIM]


def _inproj(x2, sc1, sh1, g_attn, pos, invf, cmask, smask, w_all, g_q, wq, g_kv, wkv, seq):
    n, d = x2.shape
    tm = ROW_TILE
    wcols = w_all.shape[1]
    per_b = seq // tm
    row = lambda i: (i, 0)
    bat = lambda i: (i // per_b, 0, 0)
    outs = [
        jax.ShapeDtypeStruct((n, SB_WIDTH), BF16),
        jax.ShapeDtypeStruct((n, SB_WIDTH), BF16),
        jax.ShapeDtypeStruct((n, SB_WIDTH), BF16),
        jax.ShapeDtypeStruct((n, 2 * MLA_WIDTH), BF16),
        jax.ShapeDtypeStruct((n, MLA_HEADS * MLA_KV_STRIDE), BF16),
    ]
    return pl.pallas_call(
        _inproj_kernel,
        out_shape=outs,
        grid=(n // tm,),
        in_specs=[
            pl.BlockSpec((tm, d), row),
            pl.BlockSpec((1, 1, d), bat),
            pl.BlockSpec((1, 1, d), bat),
            _resident((1, d)),
            pl.BlockSpec((tm, 1), row),
            _resident((1, LANES)),
            _resident((1, LANES)),
            _resident((1, LANES)),
            _resident((d, wcols)),
            _resident((1, LORA)),
            _resident((LORA, 2 * MLA_WIDTH)),
            _resident((1, LORA)),
            _resident((LORA, 2 * MLA_WIDTH)),
        ],
        out_specs=[
            pl.BlockSpec((tm, SB_WIDTH), row),
            pl.BlockSpec((tm, SB_WIDTH), row),
            pl.BlockSpec((tm, SB_WIDTH), row),
            pl.BlockSpec((tm, 2 * MLA_WIDTH), row),
            pl.BlockSpec((tm, MLA_HEADS * MLA_KV_STRIDE), row),
        ],
        compiler_params=_cparams(("arbitrary",)),
        name="inproj",
    )(x2, sc1, sh1, g_attn, pos, invf, cmask, smask, w_all, g_q, wq, g_kv, wkv)


_NT = (((1,), (1,)), ((), ()))


def _sb_kernel(q_ref, k_ref, v_ref, o_ref, *, tile, heads):
    i = pl.program_id(2)
    row = lax.broadcasted_iota(I32, (tile, tile), 0)
    col = lax.broadcasted_iota(I32, (tile, tile), 1)
    suffix = jnp.where(row >= col, 1.0, 0.0).astype(BF16)
    strict = col < row

    def head_step(h, kt, c, acc, diag):
        hs = slice(h * HEAD_DIM, (h + 1) * HEAD_DIM)
        k0 = pl.multiple_of(kt * tile, tile)
        k = k_ref[pl.ds(k0, tile), hs]
        v = v_ref[pl.ds(k0, tile), hs]
        z = lax.dot_general(q_ref[:, hs], k, _NT, preferred_element_type=F32)
        log_beta = jnp.minimum(z, 0.0) - jnp.log(1.0 + jnp.exp(-jnp.abs(z)))
        log_keep = log_beta - z
        if diag:
            log_keep = jnp.where(strict, log_keep, 0.0)
        incl = jnp.dot(log_keep.astype(BF16), suffix, preferred_element_type=F32)
        a = jnp.exp(incl + z + c)
        if diag:
            a = jnp.where(strict, a, 0.0)
        acc = acc + jnp.dot(a.astype(BF16), v, preferred_element_type=F32)
        return c + incl[:, 0:1], acc

    def step(kt, carry, diag):
        return tuple(head_step(h, kt, *carry[h], diag) for h in range(heads))

    def live(carry):
        c_max = carry[0][0]
        for h in range(1, heads):
            c_max = jnp.maximum(c_max, carry[h][0])
        return (jnp.max(c_max) > SB_DEAD_LOG).astype(I32)

    zero = (jnp.zeros((tile, 1), F32), jnp.zeros((tile, HEAD_DIM), F32))
    carry = step(i, (zero,) * heads, True)

    def body(state):
        kt, _, cr = state
        cr = step(kt, cr, False)
        return kt - 1, live(cr), cr

    _, _, carry = lax.while_loop(lambda st: (st[0] >= 0) & (st[1] > 0), body, (i - 1, live(carry), carry))
    for h in range(heads):
        o_ref[:, h * HEAD_DIM:(h + 1) * HEAD_DIM] = carry[h][1].astype(BF16)


def _sb_attention(sbq, sbk, sbv, batch, seq):
    n = sbq.shape[0]
    t = ATTN_TILE
    hb = ATTN_HEADS_PER_STEP
    nq = seq // t
    qmap = lambda b, h, i: (b * nq + i, h)
    kmap = lambda b, h, i: (b, h)
    return pl.pallas_call(
        functools.partial(_sb_kernel, tile=t, heads=hb),
        out_shape=jax.ShapeDtypeStruct((n, SB_WIDTH), BF16),
        grid=(batch, SB_HEADS // hb, nq),
        in_specs=[
            pl.BlockSpec((t, hb * HEAD_DIM), qmap),
            pl.BlockSpec((seq, hb * HEAD_DIM), kmap),
            pl.BlockSpec((seq, hb * HEAD_DIM), kmap),
        ],
        out_specs=pl.BlockSpec((t, hb * HEAD_DIM), qmap),
        compiler_params=_cparams(("arbitrary", "arbitrary", "arbitrary")),
        name="sb_attn",
    )(sbq, sbk, sbv)


def _mla_kernel(q_ref, kv_ref, o_ref, *, tq, tk, heads):
    i = pl.program_id(2)
    row = lax.broadcasted_iota(I32, (tq, tk), 0)
    col = lax.broadcasted_iota(I32, (tq, tk), 1)
    qk = 2 * HEAD_DIM
    per_q = tq // tk

    def scores(h, kt, diag):
        k0 = pl.multiple_of(kt * tk, tk)
        k = kv_ref[pl.ds(k0, tk), h * MLA_KV_STRIDE:h * MLA_KV_STRIDE + qk]
        s = lax.dot_general(q_ref[:, h * qk:(h + 1) * qk], k, _NT, preferred_element_type=F32)
        if diag:
            s = jnp.where(col + kt * tk <= row + i * tq, s, NEG_BIG)
        return s

    def lane_fold(x, op):
        out = x[:, :LANES]
        for cblk in range(1, tk // LANES):
            out = op(out, x[:, cblk * LANES:(cblk + 1) * LANES])
        return out

    def over_keys(step, carry):
        for r in range(per_q):
            carry = step(i * per_q + r, carry, True)
        return lax.fori_loop(0, i * per_q, lambda s, cr: step(s, cr, False), carry)

    def max_step(kt, ms, diag):
        return tuple(jnp.maximum(ms[h], lane_fold(scores(h, kt, diag), jnp.maximum)) for h in range(heads))

    ms = over_keys(max_step, (jnp.full((tq, LANES), NEG_BIG, F32),) * heads)
    row_max = tuple(jnp.max(ms[h], axis=1, keepdims=True) for h in range(heads))

    def acc_step(kt, carry, diag):
        out = []
        for h in range(heads):
            l, acc = carry[h]
            k0 = pl.multiple_of(kt * tk, tk)
            v = kv_ref[pl.ds(k0, tk), h * MLA_KV_STRIDE + qk:(h + 1) * MLA_KV_STRIDE]
            p = jnp.exp(scores(h, kt, diag) - row_max[h])
            out.append((l + lane_fold(p, jnp.add),
                        acc + jnp.dot(p.astype(BF16), v, preferred_element_type=F32)))
        return tuple(out)

    carry = over_keys(acc_step, ((jnp.zeros((tq, LANES), F32), jnp.zeros((tq, HEAD_DIM), F32)),) * heads)
    for h in range(heads):
        l, acc = carry[h]
        o_ref[:, h * HEAD_DIM:(h + 1) * HEAD_DIM] = (acc / jnp.sum(l, axis=1, keepdims=True)).astype(BF16)


def _mla_attention(mq, mkv, batch, seq):
    n = mq.shape[0]
    t = MLA_Q_TILE
    hb = ATTN_HEADS_PER_STEP
    nq = seq // t
    return pl.pallas_call(
        functools.partial(_mla_kernel, tq=t, tk=ATTN_TILE, heads=hb),
        out_shape=jax.ShapeDtypeStruct((n, MLA_WIDTH), BF16),
        grid=(batch, MLA_HEADS // hb, nq),
        in_specs=[
            pl.BlockSpec((t, hb * 2 * HEAD_DIM), lambda b, h, i: (b * nq + i, h)),
            pl.BlockSpec((seq, hb * MLA_KV_STRIDE), lambda b, h, i: (b, h)),
        ],
        out_specs=pl.BlockSpec((t, hb * HEAD_DIM), lambda b, h, i: (b * nq + i, h)),
        compiler_params=_cparams(("arbitrary", "arbitrary", "arbitrary")),
        name="mla_attn",
    )(mq, mkv)


def _outproj_kernel(sb_ref, mla_ref, x_ref, g1_ref, sc_ref, sh_ref, g_ref, wo_ref, wr_ref, br_ref,
                    x1_ref, hp_ref, ii_ref, iw_ref, cnt_ref, run_ref):
    step = pl.program_id(0)
    tm, d = x_ref.shape

    @pl.when(step == 0)
    def _():
        run_ref[...] = jnp.zeros_like(run_ref)

    o = (jnp.dot(sb_ref[...], wo_ref[:SB_WIDTH, :], preferred_element_type=F32)
         + jnp.dot(mla_ref[...], wo_ref[SB_WIDTH:, :], preferred_element_type=F32))
    x1 = x_ref[...] + g1_ref[0] * o
    x1_ref[...] = x1
    h2 = _rms(x1) * g_ref[...] * sc_ref[0] + sh_ref[0]
    hb = h2.astype(BF16)
    hb32 = hb.astype(F32)
    hp_ref[...] = hb32

    h_lo = (h2 - hb32).astype(BF16)
    w = wr_ref[...]
    w_hi = w.astype(BF16)
    w_lo = (w - w_hi.astype(F32)).astype(BF16)
    logits = (jnp.dot(hb, w_hi, preferred_element_type=F32)
              + jnp.dot(h_lo, w_hi, preferred_element_type=F32)
              + jnp.dot(hb, w_lo, preferred_element_type=F32)) + br_ref[...]

    lane = lax.broadcasted_iota(I32, (tm, N_EXPERTS), 1)
    work = logits
    vals, idxs, hots = [], [], []
    for _ in range(TOP_K):
        m = jnp.max(work, axis=1, keepdims=True)
        idx = jnp.min(jnp.where(work == m, lane, N_EXPERTS), axis=1, keepdims=True)
        hot = lane == idx
        work = jnp.where(hot, -jnp.inf, work)
        vals.append(m)
        idxs.append(idx)
        hots.append(hot)
    exps = [jnp.exp(v - vals[0]) for v in vals]
    den = exps[0] + exps[1] + exps[2] + exps[3]
    sel = jnp.where(hots[0] | hots[1] | hots[2] | hots[3], 1.0, 0.0)

    r = lax.broadcasted_iota(I32, (tm, tm), 0)
    c = lax.broadcasted_iota(I32, (tm, tm), 1)
    before = jnp.where(c < r, 1.0, 0.0).astype(BF16)
    rank = jnp.dot(before, sel.astype(BF16), preferred_element_type=F32) + run_ref[...]
    run_ref[...] = run_ref[...] + jnp.sum(sel, axis=0, keepdims=True)
    cnt_ref[...] = run_ref[...]

    lane128 = lax.broadcasted_iota(I32, (tm, LANES), 1)
    info_i = jnp.zeros((tm, LANES), I32)
    info_w = jnp.zeros((tm, LANES), F32)
    for k in range(TOP_K):
        rank_k = jnp.sum(jnp.where(hots[k], rank, 0.0), axis=1, keepdims=True).astype(I32)
        info_i = jnp.where(lane128 == k, idxs[k], info_i)
        info_i = jnp.where(lane128 == TOP_K + k, rank_k, info_i)
        info_w = jnp.where(lane128 == k, exps[k] / den, info_w)
    ii_ref[...] = info_i
    iw_ref[...] = info_w


def _outproj(sb, mla, x2, g1, sc2, sh2, g_ffn, wo, w_router, b_router, seq):
    n, d = x2.shape
    tm = ROW_TILE
    per_b = seq // tm
    row = lambda i: (i, 0)
    bat = lambda i: (i // per_b, 0, 0)
    outs = [
        jax.ShapeDtypeStruct((n, d), F32),
        jax.ShapeDtypeStruct((n, d), F32),
        jax.ShapeDtypeStruct((n, LANES), I32),
        jax.ShapeDtypeStruct((n, LANES), F32),
        jax.ShapeDtypeStruct((1, N_EXPERTS), F32),
    ]
    return pl.pallas_call(
        _outproj_kernel,
        out_shape=outs,
        grid=(n // tm,),
        in_specs=[
            pl.BlockSpec((tm, SB_WIDTH), row),
            pl.BlockSpec((tm, MLA_WIDTH), row),
            pl.BlockSpec((tm, d), row),
            pl.BlockSpec((1, 1, d), bat),
            pl.BlockSpec((1, 1, d), bat),
            pl.BlockSpec((1, 1, d), bat),
            _resident((1, d)),
            _resident((SB_WIDTH + MLA_WIDTH, d)),
            _resident((d, N_EXPERTS)),
            _resident((1, N_EXPERTS)),
        ],
        out_specs=[
            pl.BlockSpec((tm, d), row),
            pl.BlockSpec((tm, d), row),
            pl.BlockSpec((tm, LANES), row),
            pl.BlockSpec((tm, LANES), row),
            pl.BlockSpec((1, N_EXPERTS), lambda i: (0, 0)),
        ],
        scratch_shapes=[pltpu.VMEM((1, N_EXPERTS), F32)],
        compiler_params=_cparams(("arbitrary",)),
        name="outproj",
    )(sb, mla, x2, g1, sc2, sh2, g_ffn, wo, w_router, b_router)


def _row_copy(src_ref, src_row, dst_ref, dst_row, sem):
    return pltpu.make_async_copy(src_ref.at[pl.ds(src_row, 1), :], dst_ref.at[pl.ds(dst_row, 1), :], sem)


def _dispatch_kernel(pad_lo_ref, pad_hi_ref, dest_ref, h_ref, xs_ref, zero_ref, sem):
    tm = h_ref.shape[0]

    @pl.when(pl.program_id(0) == 0)
    def _():
        zero_ref[...] = jnp.zeros_like(zero_ref)

        def per_expert(e, carry):
            def put(r, c):
                _row_copy(zero_ref, 0, xs_ref, r, sem).start()
                return c

            def got(r, c):
                _row_copy(zero_ref, 0, xs_ref, 0, sem).wait()
                return c

            lax.fori_loop(pad_lo_ref[e], pad_hi_ref[e], put, 0)
            lax.fori_loop(pad_lo_ref[e], pad_hi_ref[e], got, 0)
            return carry

        lax.fori_loop(0, N_EXPERTS, per_expert, 0)

    def issue(r, carry):
        for k in range(TOP_K):
            _row_copy(h_ref, r, xs_ref, dest_ref[r * TOP_K + k], sem).start()
        return carry

    def drain(r, carry):
        for k in range(TOP_K):
            _row_copy(h_ref, 0, xs_ref, 0, sem).wait()
        return carry

    lax.fori_loop(0, tm, issue, 0)
    lax.fori_loop(0, tm, drain, 0)


def _dispatch(pad_lo, pad_hi, dest_flat, hp, rows):
    n, w = hp.shape
    tm = ROW_TILE
    grid_spec = pltpu.PrefetchScalarGridSpec(
        num_scalar_prefetch=2,
        grid=(n // tm,),
        in_specs=[
            pl.BlockSpec((tm * TOP_K,), lambda i, lo, hi: (i,), memory_space=pltpu.SMEM),
            pl.BlockSpec((tm, w), lambda i, lo, hi: (i, 0)),
        ],
        out_specs=pl.BlockSpec(memory_space=pl.ANY),
        scratch_shapes=[pltpu.VMEM((8, w), F32), pltpu.SemaphoreType.DMA(())],
    )
    return pl.pallas_call(
        _dispatch_kernel,
        out_shape=jax.ShapeDtypeStruct((rows, w), F32),
        grid_spec=grid_spec,
        compiler_params=_cparams(("arbitrary",)),
        name="dispatch",
    )(pad_lo, pad_hi, dest_flat, hp)


def _moe_kernel(ie_ref, ix_ref, ic_ref, iv_ref, ns_ref, xs_ref, w1_ref, b1g_ref, b1l_ref, w2_ref, b2_ref,
                p_ref, ys_ref, xb_ref, gl_ref, w2b_ref):
    del ie_ref, ix_ref
    item = pl.program_id(0)
    nj, fc, _ = w2b_ref.shape
    chunk = ic_ref[item]

    def ffn(j, rows):
        xt = xb_ref[:rows, :]
        ug = jnp.dot(xt, gl_ref[j, :, :fc], preferred_element_type=F32) + b1g_ref[j]
        ul = jnp.dot(xt, gl_ref[j, :, fc:], preferred_element_type=F32) + b1l_ref[j]
        glu = jnp.minimum(ug, SWIGLU_LIMIT)
        lin = jnp.clip(ul, -SWIGLU_LIMIT, SWIGLU_LIMIT)
        act = glu * jax.nn.sigmoid(SWIGLU_ALPHA * glu) * (lin + 1.0)
        ys_ref[:rows, :] += jnp.dot(act.astype(BF16), w2b_ref[j], preferred_element_type=F32)

    def for_used_rows(fn):
        for n_sub in range(1, MOE_TILE // MOE_SUB + 1):
            pl.when(ns_ref[item] == n_sub)(functools.partial(fn, n_sub * MOE_SUB))

    @pl.when(iv_ref[item] == 1)
    def _():
        @pl.when((chunk == 0) | (chunk == nj))
        def _():
            xb_ref[...] = xs_ref[...].astype(BF16)
            ys_ref[...] = jnp.broadcast_to(b2_ref[...], ys_ref.shape)

        @pl.when(chunk < nj)
        def _():
            p = p_ref[...]
            for blk in range(2 * fc // 256):
                wb = w1_ref[:, blk * 256:(blk + 1) * 256].astype(BF16)
                split = jnp.dot(wb, p, preferred_element_type=F32).astype(BF16)
                gl_ref[chunk, :, blk * LANES:(blk + 1) * LANES] = split[:, :LANES]
                gl_ref[chunk, :, fc + blk * LANES:fc + (blk + 1) * LANES] = split[:, LANES:]
            w2b_ref[chunk] = w2_ref[...].astype(BF16)
            for_used_rows(functools.partial(ffn, chunk))

        @pl.when(chunk == nj)
        def _():
            def all_chunks(rows):
                def body(j, carry):
                    ffn(j, rows)
                    return carry

                lax.fori_loop(0, nj, body, 0)

            for_used_rows(all_chunks)


def _moe_items(tile_e, tile_v, tile_f, tile_ns, nj):
    n_tiles = tile_v.shape[0]
    n_items = n_tiles + N_EXPERTS * (nj - 1)
    cost = jnp.where(tile_f == 1, nj, 1) * tile_v
    first_item = jnp.cumsum(cost) - cost
    total = jnp.sum(cost)
    last_tile = jnp.maximum(jnp.sum(tile_v) - 1, 0)
    item = jnp.arange(n_items, dtype=I32)
    valid = item < total
    tile = jnp.where(valid, jnp.sum(first_item[None, :] <= item[:, None], axis=1) - 1, last_tile).astype(I32)
    chunk = jnp.where(valid & (tile_f[tile] == 1), item - first_item[tile], nj).astype(I32)
    return tile_e[tile], tile, chunk, valid.astype(I32), (tile_ns[tile] * valid).astype(I32)


def _moe(item_e, item_x, item_c, item_v, item_ns, xs, w1, b1g, b1l, w2, b2, perm):
    rows, d = xs.shape
    f = w1.shape[2] // 2
    fc = MOE_FC
    nj = f // fc

    def wj(q, ic):
        return jnp.minimum(ic[q], nj - 1)

    grid_spec = pltpu.PrefetchScalarGridSpec(
        num_scalar_prefetch=5,
        grid=(item_e.shape[0],),
        in_specs=[
            pl.BlockSpec((MOE_TILE, d), lambda q, ie, ix, ic, iv, ns: (ix[q], 0)),
            pl.BlockSpec((None, d, 2 * fc), lambda q, ie, ix, ic, iv, ns: (ie[q], 0, wj(q, ic))),
            pl.BlockSpec((None, nj, 1, fc), lambda q, ie, ix, ic, iv, ns: (ie[q], 0, 0, 0)),
            pl.BlockSpec((None, nj, 1, fc), lambda q, ie, ix, ic, iv, ns: (ie[q], 0, 0, 0)),
            pl.BlockSpec((None, fc, d), lambda q, ie, ix, ic, iv, ns: (ie[q], wj(q, ic), 0)),
            pl.BlockSpec((None, 1, d), lambda q, ie, ix, ic, iv, ns: (ie[q], 0, 0)),
            pl.BlockSpec((256, 256), lambda q, ie, ix, ic, iv, ns: (0, 0)),
        ],
        out_specs=pl.BlockSpec((MOE_TILE, d), lambda q, ie, ix, ic, iv, ns: (ix[q], 0)),
        scratch_shapes=[
            pltpu.VMEM((MOE_TILE, d), BF16),
            pltpu.VMEM((nj, d, 2 * fc), BF16),
            pltpu.VMEM((nj, fc, d), BF16),
        ],
    )
    return pl.pallas_call(
        _moe_kernel,
        out_shape=jax.ShapeDtypeStruct((rows, d), F32),
        grid_spec=grid_spec,
        compiler_params=_cparams(("arbitrary",)),
        name="moe",
    )(item_e, item_x, item_c, item_v, item_ns, xs, w1, b1g, b1l, w2, b2, perm)


def _combine_kernel(dest_ref, iw_ref, x1_ref, g2_ref, gf_ref, ys_ref, o_ref, buf_ref, sem):
    tm = x1_ref.shape[0]

    def issue(r, carry):
        for k in range(TOP_K):
            _row_copy(ys_ref, dest_ref[r * TOP_K + k], buf_ref.at[k], r, sem).start()
        return carry

    def drain(r, carry):
        for k in range(TOP_K):
            _row_copy(ys_ref, 0, buf_ref.at[k], 0, sem).wait()
        return carry

    lax.fori_loop(0, tm, issue, 0)
    lax.fori_loop(0, tm, drain, 0)

    w = iw_ref[...]
    moe = w[:, 0:1] * buf_ref[0]
    for k in range(1, TOP_K):
        moe = moe + w[:, k:k + 1] * buf_ref[k]
    x = x1_ref[...] + g2_ref[0] * moe
    o_ref[...] = _rms(x) * gf_ref[...]


def _combine(dest_flat, info_w, x1, g2, g_final, ys, seq):
    n, d = x1.shape
    tm = COMBINE_TILE
    per_b = seq // tm
    return pl.pallas_call(
        _combine_kernel,
        out_shape=jax.ShapeDtypeStruct((n, d), F32),
        grid=(n // tm,),
        in_specs=[
            pl.BlockSpec((tm * TOP_K,), lambda i: (i,), memory_space=pltpu.SMEM),
            pl.BlockSpec((tm, LANES), lambda i: (i, 0)),
            pl.BlockSpec((tm, d), lambda i: (i, 0)),
            pl.BlockSpec((1, 1, d), lambda i: (i // per_b, 0, 0)),
            _resident((1, d)),
            pl.BlockSpec(memory_space=pl.ANY),
        ],
        out_specs=pl.BlockSpec((tm, d), lambda i: (i, 0)),
        scratch_shapes=[pltpu.VMEM((TOP_K, tm, d), F32), pltpu.SemaphoreType.DMA(())],
        compiler_params=_cparams(("arbitrary",)),
        name="combine",
    )(dest_flat, info_w, x1, g2, g_final, ys)


def _rope_lane_tables():
    inv_freq = 1.0 / (ROPE_THETA ** (jnp.arange(ROPE_HALF, dtype=F32) * 2.0 / ROPE_DIM))
    z = jnp.zeros((ROPE_HALF,), F32)
    o = jnp.ones((ROPE_HALF,), F32)
    invf = jnp.concatenate([inv_freq, z, inv_freq, z])[None, :]
    cmask = jnp.concatenate([o, z, o, z])[None, :]
    smask = jnp.concatenate([-o, z, o, z])[None, :]
    return invf, cmask, smask


def _pad_rope_cols(w):
    z = jnp.zeros(w.shape[:-1] + (ROPE_HALF,), w.dtype)
    return jnp.concatenate([w[..., :ROPE_HALF], z, w[..., ROPE_HALF:], z], axis=-1)


def kernel(x, c, positions, g_attn, w_mod, b_mod, w_in, g_q_lat, w_q_up, g_kv_lat, w_kv_up, w_out, g_ffn,
           w_router, b_router, w1, b1, w2, b2, g_final):
    batch, seq, d = x.shape
    n = batch * seq
    depth = w_mod.shape[0]
    assert depth == 1 and seq % ATTN_TILE == 0 and seq % ROW_TILE == 0 and d == SB_WIDTH + MLA_WIDTH
    x2 = x.reshape(n, d)

    c_pad = jnp.zeros((8, d), F32).at[:batch].set(c)
    mod = _mod(c_pad, w_mod[0], b_mod[0][None, :])[:batch]
    shift1, scale1, gate1, shift2, scale2, gate2 = [m[:, None, :] for m in jnp.split(mod, 6, axis=-1)]

    wi = w_in[0]
    o = 3 * SB_WIDTH + 2 * LORA
    w_all = jnp.concatenate([wi[:, :o], _pad_rope_cols(wi[:, o:])], axis=1).astype(BF16)
    wq = w_q_up[0].reshape(LORA, MLA_HEADS, MLA_QK_DIM)
    wq = jnp.concatenate([wq[..., :HEAD_DIM], _pad_rope_cols(wq[..., HEAD_DIM:])], axis=-1)
    wq = wq.reshape(LORA, MLA_HEADS * 2 * HEAD_DIM).astype(BF16)
    wkv = w_kv_up[0].astype(BF16)
    invf, cmask, smask = _rope_lane_tables()

    sbq, sbk, sbv, mq, mkv = _inproj(
        x2, 1.0 + scale1, shift1, g_attn, positions.reshape(n, 1), invf, cmask, smask, w_all,
        g_q_lat, wq, g_kv_lat, wkv, seq)

    sb_out = _sb_attention(sbq, sbk, sbv, batch, seq)
    mla_out = _mla_attention(mq, mkv, batch, seq)

    x1, hp, info_i, info_w, counts = _outproj(
        sb_out, mla_out, x2, gate1, 1.0 + scale2, shift2, g_ffn, w_out[0].astype(BF16),
        w_router[0], b_router, seq)

    counts = counts[0].astype(I32)
    padded = (counts + MOE_TILE - 1) // MOE_TILE * MOE_TILE
    pend = jnp.cumsum(padded)
    pstart = pend - padded
    idx = info_i[:, :TOP_K]
    dest = (pstart[idx] + info_i[:, TOP_K:2 * TOP_K]).reshape(n * TOP_K)
    rows = n * TOP_K + N_EXPERTS * MOE_TILE
    n_tiles = rows // MOE_TILE
    tile_row0 = jnp.arange(n_tiles, dtype=I32) * MOE_TILE
    tile_v = (tile_row0 < pend[-1]).astype(I32)
    last = jnp.maximum(pend[-1] // MOE_TILE - 1, 0)
    tile_x = jnp.minimum(jnp.arange(n_tiles, dtype=I32), last)
    tile_e = jnp.minimum(jnp.sum(pend[None, :] <= (tile_x * MOE_TILE)[:, None], axis=1), N_EXPERTS - 1).astype(I32)
    tile_f = (tile_x * MOE_TILE == pstart[tile_e]).astype(I32) * tile_v
    used = counts[tile_e] - (tile_x * MOE_TILE - pstart[tile_e])
    tile_ns = jnp.clip((used + MOE_SUB - 1) // MOE_SUB, 0, MOE_TILE // MOE_SUB) * tile_v

    xs = _dispatch(pstart + counts, pend, dest, hp, rows)

    src = jnp.arange(256)
    perm = (jnp.arange(256)[None, :] == (src % 2 * LANES + src // 2)[:, None]).astype(BF16)
    f = w1.shape[-1] // 2
    nj = f // MOE_FC
    b1r = b1[0].reshape(N_EXPERTS, nj, 1, MOE_FC, 2)
    items = _moe_items(tile_e, tile_v, tile_f, tile_ns, nj)
    ys = _moe(*items, xs, w1[0], b1r[..., 0], b1r[..., 1], w2[0], b2[0][:, None, :], perm)

    out = _combine(dest, info_w, x1, gate2, g_final[None, :], ys, seq)
    return out.reshape(batch, seq, d)
```

```python
import functools

import jax
import jax.numpy as jnp
from jax import lax
from jax.experimental import pallas as pl
from jax.experimental.pallas import tpu as pltpu

F32 = jnp.float32
BF16 = jnp.bfloat16
I32 = jnp.int32

SB_HEADS = 8
MLA_HEADS = 8
HEAD_DIM = 128
ROPE_DIM = 64
ROPE_HALF = ROPE_DIM // 2
MLA_QK_DIM = HEAD_DIM + ROPE_DIM
LORA = 512
ROPE_THETA = 10000.0
N_EXPERTS = 32
TOP_K = 4
SWIGLU_ALPHA = 1.702
SWIGLU_LIMIT = 7.0
NORM_EPS = 1e-6
SB_WIDTH = SB_HEADS * HEAD_DIM
MLA_WIDTH = MLA_HEADS * HEAD_DIM

LANES = 128
VMEM_LIMIT_BYTES = 56 * 1024 * 1024

ROW_TILE = 256
OUTPROJ_TILE = 512
ATTN_TILE = 256
MLA_Q_TILE = 512
ATTN_HEADS_PER_STEP = 4
MLA_KV_STRIDE = 3 * HEAD_DIM
SB_DEAD_LOG = -104.0
MOE_TILE = 512
MOE_SUB = 256
MOE_FC = 256
COMBINE_TILE = 128
NEG_BIG = -1e30


def _cparams(sem):
    return pltpu.CompilerParams(dimension_semantics=sem, vmem_limit_bytes=VMEM_LIMIT_BYTES)


def _resident(shape):
    nd = len(shape)
    return pl.BlockSpec(shape, lambda *_: (0,) * nd, pipeline_mode=pl.Buffered(1))


def _rms(v):
    return v * lax.rsqrt(jnp.mean(v * v, axis=-1, keepdims=True) + NORM_EPS)


def _mod_kernel(c_ref, w_ref, b_ref, o_ref):
    c = c_ref[...]
    a = (c * jax.nn.sigmoid(c)).astype(BF16)
    o_ref[...] = jnp.dot(a, w_ref[...].astype(BF16), preferred_element_type=F32) + b_ref[...]


def _mod(c_pad, w_mod, b_mod):
    rows, d = c_pad.shape
    n_out = w_mod.shape[1]
    tn = 1024
    return pl.pallas_call(
        _mod_kernel,
        out_shape=jax.ShapeDtypeStruct((rows, n_out), F32),
        grid=(n_out // tn,),
        in_specs=[
            pl.BlockSpec((rows, d), lambda j: (0, 0)),
            pl.BlockSpec((d, tn), lambda j: (0, j)),
            pl.BlockSpec((1, tn), lambda j: (0, j)),
        ],
        out_specs=pl.BlockSpec((rows, tn), lambda j: (0, j)),
        compiler_params=_cparams(("arbitrary",)),
        name="mod",
    )(c_pad, w_mod, b_mod)


def _inproj_kernel(x_ref, sc_ref, sh_ref, g_ref, pos_ref, invf_ref, cm_ref, sm_ref, w_ref,
                   gq_ref, wq_ref, gkv_ref, wkv_ref,
                   sbq_ref, sbk_ref, sbv_ref, mq_ref, mkv_ref):
    h = (_rms(x_ref[...]) * g_ref[...] * sc_ref[0] + sh_ref[0]).astype(BF16)
    proj = jnp.dot(h, w_ref[...], preferred_element_type=F32)

    sbq_ref[...] = (proj[:, :SB_WIDTH] * (HEAD_DIM ** -0.5)).astype(BF16)
    sbk_ref[...] = proj[:, SB_WIDTH:2 * SB_WIDTH].astype(BF16)
    sbv_ref[...] = proj[:, 2 * SB_WIDTH:3 * SB_WIDTH].astype(BF16)

    ang = pos_ref[...].astype(F32) * invf_ref[...]
    cosm = jnp.cos(ang) * cm_ref[...]
    sinm = jnp.sin(ang) * sm_ref[...]

    def rope(v):
        return v * cosm + pltpu.roll(v, LANES // 2, 1) * sinm

    o = 3 * SB_WIDTH
    q_lat = (_rms(proj[:, o:o + LORA]) * gq_ref[...]).astype(BF16)
    q = jnp.dot(q_lat, wq_ref[...], preferred_element_type=F32) * (MLA_QK_DIM ** -0.5)
    for hd in range(MLA_HEADS):
        c0 = hd * 2 * HEAD_DIM
        mq_ref[:, c0:c0 + HEAD_DIM] = q[:, c0:c0 + HEAD_DIM].astype(BF16)
        mq_ref[:, c0 + HEAD_DIM:c0 + 2 * HEAD_DIM] = rope(q[:, c0 + HEAD_DIM:c0 + 2 * HEAD_DIM]).astype(BF16)

    kv_lat = (_rms(proj[:, o + LORA:o + 2 * LORA]) * gkv_ref[...]).astype(BF16)
    kv = jnp.dot(kv_lat, wkv_ref[...], preferred_element_type=F32).astype(BF16)
    k_rope = rope(proj[:, o + 2 * LORA:o + 2 * LORA + LANES]).astype(BF16)
    for hd in range(MLA_HEADS):
        c0 = hd * MLA_KV_STRIDE
        mkv_ref[:, c0:c0 + HEAD_DIM] = kv[:, hd * 2 * HEAD_DIM:hd * 2 * HEAD_DIM + HEAD_DIM]
        mkv_ref[:, c0 + HEAD_DIM:c0 + 2 * HEAD_DIM] = k_rope
        mkv_ref[:, c0 + 2 * HEAD_DIM:c0 + 3 * HEAD_DIM] = kv[:, hd * 2 * HEAD_DIM + HEAD_DIM:(hd + 1) * 2 * HEAD_DIM]


def _inproj(x2, sc1, sh1, g_attn, pos, invf, cmask, smask, w_all, g_q, wq, g_kv, wkv, seq):
    n, d = x2.shape
    tm = ROW_TILE
    wcols = w_all.shape[1]
    per_b = seq // tm
    row = lambda i: (i, 0)
    bat = lambda i: (i // per_b, 0, 0)
    outs = [
        jax.ShapeDtypeStruct((n, SB_WIDTH), BF16),
        jax.ShapeDtypeStruct((n, SB_WIDTH), BF16),
        jax.ShapeDtypeStruct((n, SB_WIDTH), BF16),
        jax.ShapeDtypeStruct((n, 2 * MLA_WIDTH), BF16),
        jax.ShapeDtypeStruct((n, MLA_HEADS * MLA_KV_STRIDE), BF16),
    ]
    return pl.pallas_call(
        _inproj_kernel,
        out_shape=outs,
        grid=(n // tm,),
        in_specs=[
            pl.BlockSpec((tm, d), row),
            pl.BlockSpec((1, 1, d), bat),
            pl.BlockSpec((1, 1, d), bat),
            _resident((1, d)),
            pl.BlockSpec((tm, 1), row),
            _resident((1, LANES)),
            _resident((1, LANES)),
            _resident((1, LANES)),
            _resident((d, wcols)),
            _resident((1, LORA)),
            _resident((LORA, 2 * MLA_WIDTH)),
            _resident((1, LORA)),
            _resident((LORA, 2 * MLA_WIDTH)),
        ],
        out_specs=[
            pl.BlockSpec((tm, SB_WIDTH), row),
            pl.BlockSpec((tm, SB_WIDTH), row),
            pl.BlockSpec((tm, SB_WIDTH), row),
            pl.BlockSpec((tm, 2 * MLA_WIDTH), row),
            pl.BlockSpec((tm, MLA_HEADS * MLA_KV_STRIDE), row),
        ],
        compiler_params=_cparams(("arbitrary",)),
        name="inproj",
    )(x2, sc1, sh1, g_attn, pos, invf, cmask, smask, w_all, g_q, wq, g_kv, wkv)


_NT = (((1,), (1,)), ((), ()))


def _sb_kernel(q_ref, k_ref, v_ref, o_ref, *, tile, heads):
    i = pl.program_id(2)
    row = lax.broadcasted_iota(I32, (tile, tile), 0)
    col = lax.broadcasted_iota(I32, (tile, tile), 1)
    suffix = jnp.where(row >= col, 1.0, 0.0).astype(BF16)
    strict = col < row

    def head_step(h, kt, c, acc, diag):
        hs = slice(h * HEAD_DIM, (h + 1) * HEAD_DIM)
        k0 = pl.multiple_of(kt * tile, tile)
        k = k_ref[pl.ds(k0, tile), hs]
        v = v_ref[pl.ds(k0, tile), hs]
        z = lax.dot_general(q_ref[:, hs], k, _NT, preferred_element_type=F32)
        log_beta = jnp.minimum(z, 0.0) - jnp.log(1.0 + jnp.exp(-jnp.abs(z)))
        log_keep = log_beta - z
        if diag:
            log_keep = jnp.where(strict, log_keep, 0.0)
        incl = jnp.dot(log_keep.astype(BF16), suffix, preferred_element_type=F32)
        a = jnp.exp(incl + z + c)
        if diag:
            a = jnp.where(strict, a, 0.0)
        acc = acc + jnp.dot(a.astype(BF16), v, preferred_element_type=F32)
        return c + incl[:, 0:1], acc

    def step(kt, carry, diag):
        return tuple(head_step(h, kt, *carry[h], diag) for h in range(heads))

    def live(carry):
        c_max = carry[0][0]
        for h in range(1, heads):
            c_max = jnp.maximum(c_max, carry[h][0])
        return (jnp.max(c_max) > SB_DEAD_LOG).astype(I32)

    zero = (jnp.zeros((tile, 1), F32), jnp.zeros((tile, HEAD_DIM), F32))
    carry = step(i, (zero,) * heads, True)

    def body(state):
        kt, _, cr = state
        cr = step(kt, cr, False)
        return kt - 1, live(cr), cr

    _, _, carry = lax.while_loop(lambda st: (st[0] >= 0) & (st[1] > 0), body, (i - 1, live(carry), carry))
    for h in range(heads):
        o_ref[:, h * HEAD_DIM:(h + 1) * HEAD_DIM] = carry[h][1].astype(BF16)


def _sb_attention(sbq, sbk, sbv, batch, seq):
    n = sbq.shape[0]
    t = ATTN_TILE
    hb = ATTN_HEADS_PER_STEP
    nq = seq // t
    qmap = lambda b, h, i: (b * nq + i, h)
    kmap = lambda b, h, i: (b, h)
    return pl.pallas_call(
        functools.partial(_sb_kernel, tile=t, heads=hb),
        out_shape=jax.ShapeDtypeStruct((n, SB_WIDTH), BF16),
        grid=(batch, SB_HEADS // hb, nq),
        in_specs=[
            pl.BlockSpec((t, hb * HEAD_DIM), qmap),
            pl.BlockSpec((seq, hb * HEAD_DIM), kmap),
            pl.BlockSpec((seq, hb * HEAD_DIM), kmap),
        ],
        out_specs=pl.BlockSpec((t, hb * HEAD_DIM), qmap),
        compiler_params=_cparams(("arbitrary", "arbitrary", "arbitrary")),
        name="sb_attn",
    )(sbq, sbk, sbv)


def _mla_kernel(q_ref, kv_ref, o_ref, *, tq, tk, heads):
    i = pl.program_id(2)
    row = lax.broadcasted_iota(I32, (tq, tk), 0)
    col = lax.broadcasted_iota(I32, (tq, tk), 1)
    qk = 2 * HEAD_DIM
    per_q = tq // tk

    def scores(h, kt, diag):
        k0 = pl.multiple_of(kt * tk, tk)
        k = kv_ref[pl.ds(k0, tk), h * MLA_KV_STRIDE:h * MLA_KV_STRIDE + qk]
        s = lax.dot_general(q_ref[:, h * qk:(h + 1) * qk], k, _NT, preferred_element_type=F32)
        if diag:
            s = jnp.where(col + kt * tk <= row + i * tq, s, NEG_BIG)
        return s

    def lane_fold(x, op):
        out = x[:, :LANES]
        for cblk in range(1, tk // LANES):
            out = op(out, x[:, cblk * LANES:(cblk + 1) * LANES])
        return out

    def over_keys(step, carry):
        for r in range(per_q):
            carry = step(i * per_q + r, carry, True)
        return lax.fori_loop(0, i * per_q, lambda s, cr: step(s, cr, False), carry)

    def max_step(kt, ms, diag):
        return tuple(jnp.maximum(ms[h], lane_fold(scores(h, kt, diag), jnp.maximum)) for h in range(heads))

    ms = over_keys(max_step, (jnp.full((tq, LANES), NEG_BIG, F32),) * heads)
    row_max = tuple(jnp.max(ms[h], axis=1, keepdims=True) for h in range(heads))

    def acc_step(kt, carry, diag):
        out = []
        for h in range(heads):
            l, acc = carry[h]
            k0 = pl.multiple_of(kt * tk, tk)
            v = kv_ref[pl.ds(k0, tk), h * MLA_KV_STRIDE + qk:(h + 1) * MLA_KV_STRIDE]
            p = jnp.exp(scores(h, kt, diag) - row_max[h])
            out.append((l + lane_fold(p, jnp.add),
                        acc + jnp.dot(p.astype(BF16), v, preferred_element_type=F32)))
        return tuple(out)

    carry = over_keys(acc_step, ((jnp.zeros((tq, LANES), F32), jnp.zeros((tq, HEAD_DIM), F32)),) * heads)
    for h in range(heads):
        l, acc = carry[h]
        o_ref[:, h * HEAD_DIM:(h + 1) * HEAD_DIM] = (acc / jnp.sum(l, axis=1, keepdims=True)).astype(BF16)


def _mla_attention(mq, mkv, batch, seq):
    n = mq.shape[0]
    t = MLA_Q_TILE
    hb = ATTN_HEADS_PER_STEP
    nq = seq // t
    return pl.pallas_call(
        functools.partial(_mla_kernel, tq=t, tk=ATTN_TILE, heads=hb),
        out_shape=jax.ShapeDtypeStruct((n, MLA_WIDTH), BF16),
        grid=(batch, MLA_HEADS // hb, nq),
        in_specs=[
            pl.BlockSpec((t, hb * 2 * HEAD_DIM), lambda b, h, i: (b * nq + i, h)),
            pl.BlockSpec((seq, hb * MLA_KV_STRIDE), lambda b, h, i: (b, h)),
        ],
        out_specs=pl.BlockSpec((t, hb * HEAD_DIM), lambda b, h, i: (b * nq + i, h)),
        compiler_params=_cparams(("arbitrary", "arbitrary", "arbitrary")),
        name="mla_attn",
    )(mq, mkv)


def _outproj_kernel(sb_ref, mla_ref, x_ref, g1_ref, sc_ref, sh_ref, g_ref, wo_ref, wr_ref, br_ref,
                    x1_ref, hp_ref, ii_ref, iw_ref, cnt_ref, run_ref):
    step = pl.program_id(0)
    tm, d = x_ref.shape

    @pl.when(step == 0)
    def _():
        run_ref[...] = jnp.zeros_like(run_ref)

    o = (jnp.dot(sb_ref[...], wo_ref[:SB_WIDTH, :], preferred_element_type=F32)
         + jnp.dot(mla_ref[...], wo_ref[SB_WIDTH:, :], preferred_element_type=F32))
    x1 = x_ref[...] + g1_ref[0] * o
    x1_ref[...] = x1
    h2 = _rms(x1) * g_ref[...] * sc_ref[0] + sh_ref[0]
    hb = h2.astype(BF16)
    hb32 = hb.astype(F32)
    hp_ref[...] = hb32

    h_lo = (h2 - hb32).astype(BF16)
    w = wr_ref[...]
    w_hi = w.astype(BF16)
    w_lo = (w - w_hi.astype(F32)).astype(BF16)
    logits = (lax.dot_general(w_hi, hb, _NT, preferred_element_type=F32)
              + lax.dot_general(w_hi, h_lo, _NT, preferred_element_type=F32)
              + lax.dot_general(w_lo, hb, _NT, preferred_element_type=F32)) + br_ref[...]

    expert = lax.broadcasted_iota(I32, (N_EXPERTS, tm), 0)
    work = logits
    vals, idxs, hots = [], [], []
    for _ in range(TOP_K):
        m = jnp.max(work, axis=0, keepdims=True)
        idx = jnp.min(jnp.where(work == m, expert, N_EXPERTS), axis=0, keepdims=True)
        hot = expert == idx
        work = jnp.where(hot, -jnp.inf, work)
        vals.append(m)
        idxs.append(idx)
        hots.append(hot)
    exps = [jnp.exp(v - vals[0]) for v in vals]
    den = exps[0] + exps[1] + exps[2] + exps[3]
    sel = jnp.where(hots[0] | hots[1] | hots[2] | hots[3], 1.0, 0.0)

    r = lax.broadcasted_iota(I32, (tm, tm), 0)
    c = lax.broadcasted_iota(I32, (tm, tm), 1)
    earlier = jnp.where(r < c, 1.0, 0.0).astype(BF16)
    rank = jnp.dot(sel.astype(BF16), earlier, preferred_element_type=F32) + run_ref[...]
    run_ref[...] = run_ref[...] + jnp.sum(sel, axis=1, keepdims=True)
    cnt_ref[...] = run_ref[...]

    field = lax.broadcasted_iota(I32, (2 * TOP_K, tm), 0)
    info_i = jnp.zeros((2 * TOP_K, tm), I32)
    info_w = jnp.zeros((2 * TOP_K, tm), F32)
    for k in range(TOP_K):
        rank_k = jnp.sum(jnp.where(hots[k], rank, 0.0), axis=0, keepdims=True).astype(I32)
        info_i = jnp.where(field == k, idxs[k], info_i)
        info_i = jnp.where(field == TOP_K + k, rank_k, info_i)
        info_w = jnp.where(field == k, exps[k] / den, info_w)
    ii_ref[...] = info_i
    iw_ref[...] = info_w


def _outproj(sb, mla, x2, g1, sc2, sh2, g_ffn, wo, w_router, b_router, seq):
    n, d = x2.shape
    tm = OUTPROJ_TILE
    per_b = seq // tm
    row = lambda i: (i, 0)
    bat = lambda i: (i // per_b, 0, 0)
    outs = [
        jax.ShapeDtypeStruct((n, d), F32),
        jax.ShapeDtypeStruct((n, d), F32),
        jax.ShapeDtypeStruct((2 * TOP_K, n), I32),
        jax.ShapeDtypeStruct((2 * TOP_K, n), F32),
        jax.ShapeDtypeStruct((N_EXPERTS, 1), F32),
    ]
    return pl.pallas_call(
        _outproj_kernel,
        out_shape=outs,
        grid=(n // tm,),
        in_specs=[
            pl.BlockSpec((tm, SB_WIDTH), row),
            pl.BlockSpec((tm, MLA_WIDTH), row),
            pl.BlockSpec((tm, d), row),
            pl.BlockSpec((1, 1, d), bat),
            pl.BlockSpec((1, 1, d), bat),
            pl.BlockSpec((1, 1, d), bat),
            _resident((1, d)),
            _resident((SB_WIDTH + MLA_WIDTH, d)),
            _resident((N_EXPERTS, d)),
            _resident((N_EXPERTS, 1)),
        ],
        out_specs=[
            pl.BlockSpec((tm, d), row),
            pl.BlockSpec((tm, d), row),
            pl.BlockSpec((2 * TOP_K, tm), lambda i: (0, i)),
            pl.BlockSpec((2 * TOP_K, tm), lambda i: (0, i)),
            pl.BlockSpec((N_EXPERTS, 1), lambda i: (0, 0)),
        ],
        scratch_shapes=[pltpu.VMEM((N_EXPERTS, 1), F32)],
        compiler_params=_cparams(("arbitrary",)),
        name="outproj",
    )(sb, mla, x2, g1, sc2, sh2, g_ffn, wo, w_router, b_router)


def _row_copy(src_ref, src_row, dst_ref, dst_row, sem):
    return pltpu.make_async_copy(src_ref.at[pl.ds(src_row, 1), :], dst_ref.at[pl.ds(dst_row, 1), :], sem)


def _dispatch_kernel(pad_lo_ref, pad_hi_ref, dest_ref, h_ref, xs_ref, zero_ref, sem):
    tm = h_ref.shape[0]

    @pl.when(pl.program_id(0) == 0)
    def _():
        zero_ref[...] = jnp.zeros_like(zero_ref)

        def put(r, c):
            _row_copy(zero_ref, 0, xs_ref, r, sem).start()
            return c

        def got(r, c):
            _row_copy(zero_ref, 0, xs_ref, 0, sem).wait()
            return c

        def over_pad_rows(fn):
            def per_expert(e, carry):
                return lax.fori_loop(pad_lo_ref[e], pad_hi_ref[e], fn, carry)

            lax.fori_loop(0, N_EXPERTS, per_expert, 0)

        over_pad_rows(put)
        over_pad_rows(got)

    def issue(r, carry):
        for k in range(TOP_K):
            _row_copy(h_ref, r, xs_ref, dest_ref[r * TOP_K + k], sem).start(priority=k % 2)
        return carry

    lax.fori_loop(0, tm, issue, 0)
    for k in range(TOP_K):
        pltpu.make_async_copy(h_ref, xs_ref.at[pl.ds(0, tm), :], sem).wait()


def _dispatch(pad_lo, pad_hi, dest_flat, hp, rows):
    n, w = hp.shape
    tm = ROW_TILE
    grid_spec = pltpu.PrefetchScalarGridSpec(
        num_scalar_prefetch=2,
        grid=(n // tm,),
        in_specs=[
            pl.BlockSpec((tm * TOP_K,), lambda i, lo, hi: (i,), memory_space=pltpu.SMEM),
            pl.BlockSpec((tm, w), lambda i, lo, hi: (i, 0)),
        ],
        out_specs=pl.BlockSpec(memory_space=pl.ANY),
        scratch_shapes=[pltpu.VMEM((8, w), F32), pltpu.SemaphoreType.DMA(())],
    )
    return pl.pallas_call(
        _dispatch_kernel,
        out_shape=jax.ShapeDtypeStruct((rows, w), F32),
        grid_spec=grid_spec,
        compiler_params=_cparams(("arbitrary",)),
        name="dispatch",
    )(pad_lo, pad_hi, dest_flat, hp)


def _moe_kernel(ie_ref, ix_ref, ic_ref, iv_ref, ns_ref, xs_ref, w1_ref, b1g_ref, b1l_ref, w2_ref, b2_ref,
                p_ref, ys_ref, xb_ref, gl_ref, w2b_ref):
    del ie_ref, ix_ref
    item = pl.program_id(0)
    nj, fc, _ = w2b_ref.shape
    chunk = ic_ref[item]

    def ffn(j, rows):
        xt = xb_ref[:rows, :]
        ug = jnp.dot(xt, gl_ref[j, :, :fc], preferred_element_type=F32) + b1g_ref[j]
        ul = jnp.dot(xt, gl_ref[j, :, fc:], preferred_element_type=F32) + b1l_ref[j]
        glu = jnp.minimum(ug, SWIGLU_LIMIT)
        lin = jnp.clip(ul, -SWIGLU_LIMIT, SWIGLU_LIMIT)
        act = glu * jax.nn.sigmoid(SWIGLU_ALPHA * glu) * (lin + 1.0)
        ys_ref[:rows, :] += jnp.dot(act.astype(BF16), w2b_ref[j], preferred_element_type=F32)

    def for_used_rows(fn):
        for n_sub in range(1, MOE_TILE // MOE_SUB + 1):
            pl.when(ns_ref[item] == n_sub)(functools.partial(fn, n_sub * MOE_SUB))

    @pl.when(iv_ref[item] == 1)
    def _():
        @pl.when((chunk == 0) | (chunk == nj))
        def _():
            def load_rows(rows):
                xb_ref[:rows, :] = xs_ref[:rows, :].astype(BF16)

            for_used_rows(load_rows)
            ys_ref[...] = jnp.broadcast_to(b2_ref[...], ys_ref.shape)

        @pl.when(chunk < nj)
        def _():
            p = p_ref[...]
            for blk in range(2 * fc // 256):
                wb = w1_ref[:, blk * 256:(blk + 1) * 256].astype(BF16)
                split = jnp.dot(wb, p, preferred_element_type=F32).astype(BF16)
                gl_ref[chunk, :, blk * LANES:(blk + 1) * LANES] = split[:, :LANES]
                gl_ref[chunk, :, fc + blk * LANES:fc + (blk + 1) * LANES] = split[:, LANES:]
            w2b_ref[chunk] = w2_ref[...].astype(BF16)
            for_used_rows(functools.partial(ffn, chunk))

        @pl.when(chunk == nj)
        def _():
            def all_chunks(rows):
                def body(j, carry):
                    ffn(j, rows)
                    return carry

                lax.fori_loop(0, nj, body, 0)

            for_used_rows(all_chunks)


def _moe_items(tile_e, tile_v, tile_f, tile_ns, nj):
    n_tiles = tile_v.shape[0]
    n_items = n_tiles + N_EXPERTS * (nj - 1)
    cost = jnp.where(tile_f == 1, nj, 1) * tile_v
    first_item = jnp.cumsum(cost) - cost
    total = jnp.sum(cost)
    last_tile = jnp.maximum(jnp.sum(tile_v) - 1, 0)
    item = jnp.arange(n_items, dtype=I32)
    valid = item < total
    tile = jnp.where(valid, jnp.sum(first_item[None, :] <= item[:, None], axis=1) - 1, last_tile).astype(I32)
    chunk = jnp.where(valid & (tile_f[tile] == 1), item - first_item[tile], nj).astype(I32)
    return tile_e[tile], tile, chunk, valid.astype(I32), (tile_ns[tile] * valid).astype(I32)


def _moe(item_e, item_x, item_c, item_v, item_ns, xs, w1, b1g, b1l, w2, b2, perm):
    rows, d = xs.shape
    f = w1.shape[2] // 2
    fc = MOE_FC
    nj = f // fc

    def wj(q, ic):
        return jnp.minimum(ic[q], nj - 1)

    grid_spec = pltpu.PrefetchScalarGridSpec(
        num_scalar_prefetch=5,
        grid=(item_e.shape[0],),
        in_specs=[
            pl.BlockSpec((MOE_TILE, d), lambda q, ie, ix, ic, iv, ns: (ix[q], 0)),
            pl.BlockSpec((None, d, 2 * fc), lambda q, ie, ix, ic, iv, ns: (ie[q], 0, wj(q, ic))),
            pl.BlockSpec((None, nj, 1, fc), lambda q, ie, ix, ic, iv, ns: (ie[q], 0, 0, 0)),
            pl.BlockSpec((None, nj, 1, fc), lambda q, ie, ix, ic, iv, ns: (ie[q], 0, 0, 0)),
            pl.BlockSpec((None, fc, d), lambda q, ie, ix, ic, iv, ns: (ie[q], wj(q, ic), 0)),
            pl.BlockSpec((None, 1, d), lambda q, ie, ix, ic, iv, ns: (ie[q], 0, 0)),
            pl.BlockSpec((256, 256), lambda q, ie, ix, ic, iv, ns: (0, 0)),
        ],
        out_specs=pl.BlockSpec((MOE_TILE, d), lambda q, ie, ix, ic, iv, ns: (ix[q], 0)),
        scratch_shapes=[
            pltpu.VMEM((MOE_TILE, d), BF16),
            pltpu.VMEM((nj, d, 2 * fc), BF16),
            pltpu.VMEM((nj, fc, d), BF16),
        ],
    )
    return pl.pallas_call(
        _moe_kernel,
        out_shape=jax.ShapeDtypeStruct((rows, d), F32),
        grid_spec=grid_spec,
        compiler_params=_cparams(("arbitrary",)),
        name="moe",
    )(item_e, item_x, item_c, item_v, item_ns, xs, w1, b1g, b1l, w2, b2, perm)


def _combine_kernel(dest_ref, next_ref, w_ref, x1_ref, g2_ref, gf_ref, ys_ref, o_ref, buf_ref, sem):
    i = pl.program_id(0)
    tm = x1_ref.shape[0]

    def gather(idx_ref, s):
        def issue(r, carry):
            for k in range(TOP_K):
                _row_copy(ys_ref, idx_ref[r * TOP_K + k], buf_ref.at[s], k * tm + r, sem.at[s]).start(priority=k % 2)
            return carry

        lax.fori_loop(0, tm, issue, 0)

    @pl.when(i == 0)
    def _():
        gather(dest_ref, 0)

    def reduce_slot(s):
        @pl.when(i + 1 < pl.num_programs(0))
        def _():
            gather(next_ref, 1 - s)

        pltpu.make_async_copy(ys_ref.at[pl.ds(0, TOP_K * tm), :], buf_ref.at[s], sem.at[s]).wait()
        w = w_ref[...]
        moe = w[:, 0:1] * buf_ref[s, :tm, :]
        for k in range(1, TOP_K):
            moe = moe + w[:, k:k + 1] * buf_ref[s, k * tm:(k + 1) * tm, :]
        x = x1_ref[...] + g2_ref[0] * moe
        o_ref[...] = _rms(x) * gf_ref[...]

    for s in range(2):
        pl.when(i % 2 == s)(functools.partial(reduce_slot, s))


def _combine(dest_flat, weights, x1, g2, g_final, ys, seq):
    n, d = x1.shape
    tm = COMBINE_TILE
    per_b = seq // tm
    steps = n // tm
    return pl.pallas_call(
        _combine_kernel,
        out_shape=jax.ShapeDtypeStruct((n, d), F32),
        grid=(steps,),
        in_specs=[
            pl.BlockSpec((tm * TOP_K,), lambda i: (i,), memory_space=pltpu.SMEM),
            pl.BlockSpec((tm * TOP_K,), lambda i: (jnp.minimum(i + 1, steps - 1),), memory_space=pltpu.SMEM),
            pl.BlockSpec((tm, TOP_K), lambda i: (i, 0)),
            pl.BlockSpec((tm, d), lambda i: (i, 0)),
            pl.BlockSpec((1, 1, d), lambda i: (i // per_b, 0, 0)),
            _resident((1, d)),
            pl.BlockSpec(memory_space=pl.ANY),
        ],
        out_specs=pl.BlockSpec((tm, d), lambda i: (i, 0)),
        scratch_shapes=[pltpu.VMEM((2, TOP_K * tm, d), F32), pltpu.SemaphoreType.DMA((2,))],
        compiler_params=_cparams(("arbitrary",)),
        name="combine",
    )(dest_flat, dest_flat, weights, x1, g2, g_final, ys)


def _rope_lane_tables():
    inv_freq = 1.0 / (ROPE_THETA ** (jnp.arange(ROPE_HALF, dtype=F32) * 2.0 / ROPE_DIM))
    z = jnp.zeros((ROPE_HALF,), F32)
    o = jnp.ones((ROPE_HALF,), F32)
    invf = jnp.concatenate([inv_freq, z, inv_freq, z])[None, :]
    cmask = jnp.concatenate([o, z, o, z])[None, :]
    smask = jnp.concatenate([-o, z, o, z])[None, :]
    return invf, cmask, smask


def _pad_rope_cols(w):
    z = jnp.zeros(w.shape[:-1] + (ROPE_HALF,), w.dtype)
    return jnp.concatenate([w[..., :ROPE_HALF], z, w[..., ROPE_HALF:], z], axis=-1)


def kernel(x, c, positions, g_attn, w_mod, b_mod, w_in, g_q_lat, w_q_up, g_kv_lat, w_kv_up, w_out, g_ffn,
           w_router, b_router, w1, b1, w2, b2, g_final):
    batch, seq, d = x.shape
    n = batch * seq
    depth = w_mod.shape[0]
    assert depth == 1 and seq % ATTN_TILE == 0 and seq % ROW_TILE == 0 and d == SB_WIDTH + MLA_WIDTH
    x2 = x.reshape(n, d)

    c_pad = jnp.zeros((8, d), F32).at[:batch].set(c)
    mod = _mod(c_pad, w_mod[0], b_mod[0][None, :])[:batch]
    shift1, scale1, gate1, shift2, scale2, gate2 = [m[:, None, :] for m in jnp.split(mod, 6, axis=-1)]

    wi = w_in[0]
    o = 3 * SB_WIDTH + 2 * LORA
    w_all = jnp.concatenate([wi[:, :o], _pad_rope_cols(wi[:, o:])], axis=1).astype(BF16)
    wq = w_q_up[0].reshape(LORA, MLA_HEADS, MLA_QK_DIM)
    wq = jnp.concatenate([wq[..., :HEAD_DIM], _pad_rope_cols(wq[..., HEAD_DIM:])], axis=-1)
    wq = wq.reshape(LORA, MLA_HEADS * 2 * HEAD_DIM).astype(BF16)
    wkv = w_kv_up[0].astype(BF16)
    invf, cmask, smask = _rope_lane_tables()

    sbq, sbk, sbv, mq, mkv = _inproj(
        x2, 1.0 + scale1, shift1, g_attn, positions.reshape(n, 1), invf, cmask, smask, w_all,
        g_q_lat, wq, g_kv_lat, wkv, seq)

    sb_out = _sb_attention(sbq, sbk, sbv, batch, seq)
    mla_out = _mla_attention(mq, mkv, batch, seq)

    x1, hp, info_i, info_w, counts = _outproj(
        sb_out, mla_out, x2, gate1, 1.0 + scale2, shift2, g_ffn, w_out[0].astype(BF16),
        w_router[0].T, b_router.reshape(N_EXPERTS, 1), seq)

    counts = counts[:, 0].astype(I32)
    padded = (counts + MOE_TILE - 1) // MOE_TILE * MOE_TILE
    pend = jnp.cumsum(padded)
    pstart = pend - padded
    idx = info_i[:TOP_K].T
    dest = (pstart[idx] + info_i[TOP_K:].T).reshape(n * TOP_K)
    rows = n * TOP_K + N_EXPERTS * MOE_TILE
    n_tiles = rows // MOE_TILE
    tile_row0 = jnp.arange(n_tiles, dtype=I32) * MOE_TILE
    tile_v = (tile_row0 < pend[-1]).astype(I32)
    last = jnp.maximum(pend[-1] // MOE_TILE - 1, 0)
    tile_x = jnp.minimum(jnp.arange(n_tiles, dtype=I32), last)
    tile_e = jnp.minimum(jnp.sum(pend[None, :] <= (tile_x * MOE_TILE)[:, None], axis=1), N_EXPERTS - 1).astype(I32)
    tile_f = (tile_x * MOE_TILE == pstart[tile_e]).astype(I32) * tile_v
    used = counts[tile_e] - (tile_x * MOE_TILE - pstart[tile_e])
    tile_ns = jnp.clip((used + MOE_SUB - 1) // MOE_SUB, 0, MOE_TILE // MOE_SUB) * tile_v

    xs = _dispatch(pstart + counts, pstart + (counts + MOE_SUB - 1) // MOE_SUB * MOE_SUB, dest, hp, rows)

    src = jnp.arange(256)
    perm = (jnp.arange(256)[None, :] == (src % 2 * LANES + src // 2)[:, None]).astype(BF16)
    f = w1.shape[-1] // 2
    nj = f // MOE_FC
    b1r = b1[0].reshape(N_EXPERTS, nj, 1, MOE_FC, 2)
    items = _moe_items(tile_e, tile_v, tile_f, tile_ns, nj)
    ys = _moe(*items, xs, w1[0], b1r[..., 0], b1r[..., 1], w2[0], b2[0][:, None, :], perm)

    out = _combine(dest, info_w[:TOP_K].T, x1, gate2, g_final[None, :], ys, seq)
    return out.reshape(batch, seq, d)
```

```python
import functools

import jax
import jax.numpy as jnp
from jax import lax
from jax.experimental import pallas as pl
from jax.experimental.pallas import tpu as pltpu

F32 = jnp.float32
BF16 = jnp.bfloat16
I32 = jnp.int32

SB_HEADS = 8
MLA_HEADS = 8
HEAD_DIM = 128
ROPE_DIM = 64
ROPE_HALF = ROPE_DIM // 2
MLA_QK_DIM = HEAD_DIM + ROPE_DIM
LORA = 512
ROPE_THETA = 10000.0
N_EXPERTS = 32
TOP_K = 4
SWIGLU_ALPHA = 1.702
SWIGLU_LIMIT = 7.0
NORM_EPS = 1e-6
SB_WIDTH = SB_HEADS * HEAD_DIM
MLA_WIDTH = MLA_HEADS * HEAD_DIM

LANES = 128
VMEM_LIMIT_BYTES = 56 * 1024 * 1024

ROW_TILE = 256
OUTPROJ_TILE = 512
ATTN_TILE = 256
MLA_Q_TILE = 512
MLA_K_TILE = 512
ATTN_HEADS_PER_STEP = 4
MLA_HEADS_PER_STEP = 8
MLA_KV_STRIDE = 3 * HEAD_DIM
SB_DEAD_LOG = -104.0
MOE_TILE = 512
MOE_SUB = 256
MOE_FC = 256
COMBINE_TILE = 128
NEG_BIG = -1e30


def _cparams(sem):
    return pltpu.CompilerParams(dimension_semantics=sem, vmem_limit_bytes=VMEM_LIMIT_BYTES)


def _resident(shape):
    nd = len(shape)
    return pl.BlockSpec(shape, lambda *_: (0,) * nd, pipeline_mode=pl.Buffered(1))


def _rms(v):
    return v * lax.rsqrt(jnp.mean(v * v, axis=-1, keepdims=True) + NORM_EPS)


def _mod_kernel(c_ref, w_ref, b_ref, o_ref):
    c = c_ref[...]
    a = (c * jax.nn.sigmoid(c)).astype(BF16)
    o_ref[...] = jnp.dot(a, w_ref[...].astype(BF16), preferred_element_type=F32) + b_ref[...]


def _mod(c_pad, w_mod, b_mod):
    rows, d = c_pad.shape
    n_out = w_mod.shape[1]
    tn = 1024
    return pl.pallas_call(
        _mod_kernel,
        out_shape=jax.ShapeDtypeStruct((rows, n_out), F32),
        grid=(n_out // tn,),
        in_specs=[
            pl.BlockSpec((rows, d), lambda j: (0, 0)),
            pl.BlockSpec((d, tn), lambda j: (0, j)),
            pl.BlockSpec((1, tn), lambda j: (0, j)),
        ],
        out_specs=pl.BlockSpec((rows, tn), lambda j: (0, j)),
        compiler_params=_cparams(("arbitrary",)),
        name="mod",
    )(c_pad, w_mod, b_mod)


def _inproj_kernel(x_ref, sc_ref, sh_ref, g_ref, pos_ref, invf_ref, cm_ref, sm_ref, w_ref, wkr_ref,
                   gq_ref, wq_ref, gkv_ref, wkv_ref,
                   sbq_ref, sbk_ref, sbv_ref, mq_ref, mkv_ref):
    h = (_rms(x_ref[...]) * g_ref[...] * sc_ref[0] + sh_ref[0]).astype(BF16)
    proj = jnp.dot(h, w_ref[...], preferred_element_type=F32)

    sbq_ref[...] = (proj[:, :SB_WIDTH] * (HEAD_DIM ** -0.5)).astype(BF16)
    sbk_ref[...] = proj[:, SB_WIDTH:2 * SB_WIDTH].astype(BF16)
    sbv_ref[...] = proj[:, 2 * SB_WIDTH:3 * SB_WIDTH].astype(BF16)

    ang = pos_ref[...].astype(F32) * invf_ref[...]
    cosm = jnp.cos(ang) * cm_ref[...]
    sinm = jnp.sin(ang) * sm_ref[...]

    def rope(v):
        return v * cosm + pltpu.roll(v, LANES // 2, 1) * sinm

    o = 3 * SB_WIDTH
    q_lat = (_rms(proj[:, o:o + LORA]) * gq_ref[...]).astype(BF16)
    q = jnp.dot(q_lat, wq_ref[...], preferred_element_type=F32) * (MLA_QK_DIM ** -0.5)
    for hd in range(MLA_HEADS):
        c0 = hd * 2 * HEAD_DIM
        mq_ref[:, c0:c0 + HEAD_DIM] = q[:, c0:c0 + HEAD_DIM].astype(BF16)
        mq_ref[:, c0 + HEAD_DIM:c0 + 2 * HEAD_DIM] = rope(q[:, c0 + HEAD_DIM:c0 + 2 * HEAD_DIM]).astype(BF16)

    kv_lat = (_rms(proj[:, o + LORA:o + 2 * LORA]) * gkv_ref[...]).astype(BF16)
    kv = jnp.dot(kv_lat, wkv_ref[...], preferred_element_type=F32).astype(BF16)
    k_rope = rope(jnp.dot(h, wkr_ref[...], preferred_element_type=F32)).astype(BF16)
    for hd in range(MLA_HEADS):
        c0 = hd * MLA_KV_STRIDE
        mkv_ref[:, c0:c0 + HEAD_DIM] = kv[:, hd * 2 * HEAD_DIM:hd * 2 * HEAD_DIM + HEAD_DIM]
        mkv_ref[:, c0 + HEAD_DIM:c0 + 2 * HEAD_DIM] = k_rope
        mkv_ref[:, c0 + 2 * HEAD_DIM:c0 + 3 * HEAD_DIM] = kv[:, hd * 2 * HEAD_DIM + HEAD_DIM:(hd + 1) * 2 * HEAD_DIM]


def _inproj(x2, sc1, sh1, g_attn, pos, invf, cmask, smask, w_all, w_kr, g_q, wq, g_kv, wkv, seq):
    n, d = x2.shape
    tm = ROW_TILE
    wcols = w_all.shape[1]
    per_b = seq // tm
    row = lambda i: (i, 0)
    bat = lambda i: (i // per_b, 0, 0)
    outs = [
        jax.ShapeDtypeStruct((n, SB_WIDTH), BF16),
        jax.ShapeDtypeStruct((n, SB_WIDTH), BF16),
        jax.ShapeDtypeStruct((n, SB_WIDTH), BF16),
        jax.ShapeDtypeStruct((n, 2 * MLA_WIDTH), BF16),
        jax.ShapeDtypeStruct((n, MLA_HEADS * MLA_KV_STRIDE), BF16),
    ]
    return pl.pallas_call(
        _inproj_kernel,
        out_shape=outs,
        grid=(n // tm,),
        in_specs=[
            pl.BlockSpec((tm, d), row),
            pl.BlockSpec((1, 1, d), bat),
            pl.BlockSpec((1, 1, d), bat),
            _resident((1, d)),
            pl.BlockSpec((tm, 1), row),
            _resident((1, LANES)),
            _resident((1, LANES)),
            _resident((1, LANES)),
            _resident((d, wcols)),
            _resident((d, LANES)),
            _resident((1, LORA)),
            _resident((LORA, 2 * MLA_WIDTH)),
            _resident((1, LORA)),
            _resident((LORA, 2 * MLA_WIDTH)),
        ],
        out_specs=[
            pl.BlockSpec((tm, SB_WIDTH), row),
            pl.BlockSpec((tm, SB_WIDTH), row),
            pl.BlockSpec((tm, SB_WIDTH), row),
            pl.BlockSpec((tm, 2 * MLA_WIDTH), row),
            pl.BlockSpec((tm, MLA_HEADS * MLA_KV_STRIDE), row),
        ],
        compiler_params=_cparams(("arbitrary",)),
        name="inproj",
    )(x2, sc1, sh1, g_attn, pos, invf, cmask, smask, w_all, w_kr, g_q, wq, g_kv, wkv)


_NT = (((1,), (1,)), ((), ()))


def _sb_kernel(q_ref, k_ref, v_ref, o_ref, *, tile, heads):
    i = pl.program_id(2)
    row = lax.broadcasted_iota(I32, (tile, tile), 0)
    col = lax.broadcasted_iota(I32, (tile, tile), 1)
    suffix = jnp.where(row >= col, 1.0, 0.0).astype(BF16)
    strict = col < row

    def head_step(h, kt, c, acc, diag):
        hs = slice(h * HEAD_DIM, (h + 1) * HEAD_DIM)
        k0 = pl.multiple_of(kt * tile, tile)
        k = k_ref[pl.ds(k0, tile), hs]
        v = v_ref[pl.ds(k0, tile), hs]
        z = lax.dot_general(q_ref[:, hs], k, _NT, preferred_element_type=F32)
        log_beta = jnp.minimum(z, 0.0) - jnp.log(1.0 + jnp.exp(-jnp.abs(z)))
        log_keep = log_beta - z
        if diag:
            log_keep = jnp.where(strict, log_keep, 0.0)
        incl = jnp.dot(log_keep.astype(BF16), suffix, preferred_element_type=F32)
        a = jnp.exp(incl + z + c)
        if diag:
            a = jnp.where(strict, a, 0.0)
        acc = acc + jnp.dot(a.astype(BF16), v, preferred_element_type=F32)
        return c + incl[:, 0:1], acc

    def step(kt, carry, diag):
        return tuple(head_step(h, kt, *carry[h], diag) for h in range(heads))

    def live(carry):
        c_max = carry[0][0]
        for h in range(1, heads):
            c_max = jnp.maximum(c_max, carry[h][0])
        return (jnp.max(c_max) > SB_DEAD_LOG).astype(I32)

    zero = (jnp.zeros((tile, 1), F32), jnp.zeros((tile, HEAD_DIM), F32))
    carry = step(i, (zero,) * heads, True)

    def body(state):
        kt, _, cr = state
        cr = step(kt, cr, False)
        return kt - 1, live(cr), cr

    _, _, carry = lax.while_loop(lambda st: (st[0] >= 0) & (st[1] > 0), body, (i - 1, live(carry), carry))
    for h in range(heads):
        o_ref[:, h * HEAD_DIM:(h + 1) * HEAD_DIM] = carry[h][1].astype(BF16)


def _sb_attention(sbq, sbk, sbv, batch, seq):
    n = sbq.shape[0]
    t = ATTN_TILE
    hb = ATTN_HEADS_PER_STEP
    nq = seq // t
    qmap = lambda b, h, i: (b * nq + i, h)
    kmap = lambda b, h, i: (b, h)
    return pl.pallas_call(
        functools.partial(_sb_kernel, tile=t, heads=hb),
        out_shape=jax.ShapeDtypeStruct((n, SB_WIDTH), BF16),
        grid=(batch, SB_HEADS // hb, nq),
        in_specs=[
            pl.BlockSpec((t, hb * HEAD_DIM), qmap),
            pl.BlockSpec((seq, hb * HEAD_DIM), kmap),
            pl.BlockSpec((seq, hb * HEAD_DIM), kmap),
        ],
        out_specs=pl.BlockSpec((t, hb * HEAD_DIM), qmap),
        compiler_params=_cparams(("arbitrary", "arbitrary", "arbitrary")),
        name="sb_attn",
    )(sbq, sbk, sbv)


def _mla_kernel(q_ref, kv_ref, o_ref, *, tq, tk, heads):
    i = pl.program_id(2)
    row = lax.broadcasted_iota(I32, (tq, tk), 0)
    col = lax.broadcasted_iota(I32, (tq, tk), 1)
    qk = 2 * HEAD_DIM
    per_q = tq // tk

    def scores(h, kt, diag):
        k0 = pl.multiple_of(kt * tk, tk)
        k = kv_ref[pl.ds(k0, tk), h * MLA_KV_STRIDE:h * MLA_KV_STRIDE + qk]
        s = lax.dot_general(q_ref[:, h * qk:(h + 1) * qk], k, _NT, preferred_element_type=F32)
        if diag:
            s = jnp.where(col + kt * tk <= row + i * tq, s, NEG_BIG)
        return s

    def lane_fold(x, op):
        out = x[:, :LANES]
        for cblk in range(1, tk // LANES):
            out = op(out, x[:, cblk * LANES:(cblk + 1) * LANES])
        return out

    def over_keys(step, carry):
        for r in range(per_q):
            carry = step(i * per_q + r, carry, True)
        return lax.fori_loop(0, i * per_q, lambda s, cr: step(s, cr, False), carry)

    def max_step(kt, ms, diag):
        return tuple(jnp.maximum(ms[h], lane_fold(scores(h, kt, diag), jnp.maximum)) for h in range(heads))

    ms = over_keys(max_step, (jnp.full((tq, LANES), NEG_BIG, F32),) * heads)
    row_max = tuple(jnp.max(ms[h], axis=1, keepdims=True) for h in range(heads))

    def acc_step(kt, carry, diag):
        out = []
        for h in range(heads):
            l, acc = carry[h]
            k0 = pl.multiple_of(kt * tk, tk)
            v = kv_ref[pl.ds(k0, tk), h * MLA_KV_STRIDE + qk:(h + 1) * MLA_KV_STRIDE]
            p = jnp.exp(scores(h, kt, diag) - row_max[h])
            out.append((l + lane_fold(p, jnp.add),
                        acc + jnp.dot(p.astype(BF16), v, preferred_element_type=F32)))
        return tuple(out)

    carry = over_keys(acc_step, ((jnp.zeros((tq, LANES), F32), jnp.zeros((tq, HEAD_DIM), F32)),) * heads)
    for h in range(heads):
        l, acc = carry[h]
        o_ref[:, h * HEAD_DIM:(h + 1) * HEAD_DIM] = (acc / jnp.sum(l, axis=1, keepdims=True)).astype(BF16)


def _mla_attention(mq, mkv, batch, seq):
    n = mq.shape[0]
    t = MLA_Q_TILE
    hb = MLA_HEADS_PER_STEP
    nq = seq // t
    return pl.pallas_call(
        functools.partial(_mla_kernel, tq=t, tk=MLA_K_TILE, heads=hb),
        out_shape=jax.ShapeDtypeStruct((n, MLA_WIDTH), BF16),
        grid=(batch, MLA_HEADS // hb, nq),
        in_specs=[
            pl.BlockSpec((t, hb * 2 * HEAD_DIM), lambda b, h, i: (b * nq + i, h)),
            pl.BlockSpec((seq, hb * MLA_KV_STRIDE), lambda b, h, i: (b, h), pipeline_mode=pl.Buffered(1)),
        ],
        out_specs=pl.BlockSpec((t, hb * HEAD_DIM), lambda b, h, i: (b * nq + i, h)),
        compiler_params=_cparams(("arbitrary", "arbitrary", "arbitrary")),
        name="mla_attn",
    )(mq, mkv)


def _outproj_kernel(sb_ref, mla_ref, x_ref, g1_ref, sc_ref, sh_ref, g_ref, wo_ref, wr_ref, br_ref,
                    x1_ref, hp_ref, ii_ref, iw_ref, cnt_ref, run_ref):
    step = pl.program_id(0)
    tm, d = x_ref.shape

    @pl.when(step == 0)
    def _():
        run_ref[...] = jnp.zeros_like(run_ref)

    o = (jnp.dot(sb_ref[...], wo_ref[:SB_WIDTH, :], preferred_element_type=F32)
         + jnp.dot(mla_ref[...], wo_ref[SB_WIDTH:, :], preferred_element_type=F32))
    x1 = x_ref[...] + g1_ref[0] * o
    x1_ref[...] = x1
    h2 = _rms(x1) * g_ref[...] * sc_ref[0] + sh_ref[0]
    hb = h2.astype(BF16)
    hb32 = hb.astype(F32)
    hp_ref[...] = hb32

    h_lo = (h2 - hb32).astype(BF16)
    w = wr_ref[...]
    w_hi = w.astype(BF16)
    w_lo = (w - w_hi.astype(F32)).astype(BF16)
    logits = (lax.dot_general(w_hi, hb, _NT, preferred_element_type=F32)
              + lax.dot_general(w_hi, h_lo, _NT, preferred_element_type=F32)
              + lax.dot_general(w_lo, hb, _NT, preferred_element_type=F32)) + br_ref[...]

    expert = lax.broadcasted_iota(I32, (N_EXPERTS, tm), 0)
    work = logits
    vals, idxs, hots = [], [], []
    for _ in range(TOP_K):
        m = jnp.max(work, axis=0, keepdims=True)
        idx = jnp.min(jnp.where(work == m, expert, N_EXPERTS), axis=0, keepdims=True)
        hot = expert == idx
        work = jnp.where(hot, -jnp.inf, work)
        vals.append(m)
        idxs.append(idx)
        hots.append(hot)
    exps = [jnp.exp(v - vals[0]) for v in vals]
    den = exps[0] + exps[1] + exps[2] + exps[3]
    sel = jnp.where(hots[0] | hots[1] | hots[2] | hots[3], 1.0, 0.0)

    r = lax.broadcasted_iota(I32, (tm, tm), 0)
    c = lax.broadcasted_iota(I32, (tm, tm), 1)
    earlier = jnp.where(r < c, 1.0, 0.0).astype(BF16)
    rank = jnp.dot(sel.astype(BF16), earlier, preferred_element_type=F32) + run_ref[...]
    run_ref[...] = run_ref[...] + jnp.sum(sel, axis=1, keepdims=True)
    cnt_ref[...] = run_ref[...]

    field = lax.broadcasted_iota(I32, (2 * TOP_K, tm), 0)
    info_i = jnp.zeros((2 * TOP_K, tm), I32)
    info_w = jnp.zeros((2 * TOP_K, tm), F32)
    for k in range(TOP_K):
        rank_k = jnp.sum(jnp.where(hots[k], rank, 0.0), axis=0, keepdims=True).astype(I32)
        info_i = jnp.where(field == k, idxs[k], info_i)
        info_i = jnp.where(field == TOP_K + k, rank_k, info_i)
        info_w = jnp.where(field == k, exps[k] / den, info_w)
    ii_ref[...] = info_i
    iw_ref[...] = info_w


def _outproj(sb, mla, x2, g1, sc2, sh2, g_ffn, wo, w_router, b_router, seq):
    n, d = x2.shape
    tm = OUTPROJ_TILE
    per_b = seq // tm
    row = lambda i: (i, 0)
    bat = lambda i: (i // per_b, 0, 0)
    outs = [
        jax.ShapeDtypeStruct((n, d), F32),
        jax.ShapeDtypeStruct((n, d), F32),
        jax.ShapeDtypeStruct((2 * TOP_K, n), I32),
        jax.ShapeDtypeStruct((2 * TOP_K, n), F32),
        jax.ShapeDtypeStruct((N_EXPERTS, 1), F32),
    ]
    return pl.pallas_call(
        _outproj_kernel,
        out_shape=outs,
        grid=(n // tm,),
        in_specs=[
            pl.BlockSpec((tm, SB_WIDTH), row),
            pl.BlockSpec((tm, MLA_WIDTH), row),
            pl.BlockSpec((tm, d), row),
            pl.BlockSpec((1, 1, d), bat),
            pl.BlockSpec((1, 1, d), bat),
            pl.BlockSpec((1, 1, d), bat),
            _resident((1, d)),
            _resident((SB_WIDTH + MLA_WIDTH, d)),
            _resident((N_EXPERTS, d)),
            _resident((N_EXPERTS, 1)),
        ],
        out_specs=[
            pl.BlockSpec((tm, d), row),
            pl.BlockSpec((tm, d), row),
            pl.BlockSpec((2 * TOP_K, tm), lambda i: (0, i)),
            pl.BlockSpec((2 * TOP_K, tm), lambda i: (0, i)),
            pl.BlockSpec((N_EXPERTS, 1), lambda i: (0, 0)),
        ],
        scratch_shapes=[pltpu.VMEM((N_EXPERTS, 1), F32)],
        compiler_params=_cparams(("arbitrary",)),
        name="outproj",
    )(sb, mla, x2, g1, sc2, sh2, g_ffn, wo, w_router, b_router)


def _row_copy(src_ref, src_row, dst_ref, dst_row, sem):
    return pltpu.make_async_copy(src_ref.at[pl.ds(src_row, 1), :], dst_ref.at[pl.ds(dst_row, 1), :], sem)


def _dispatch_kernel(pad_lo_ref, pad_hi_ref, dest_ref, h_ref, xs_ref, zero_ref, sem):
    tm = h_ref.shape[0]

    @pl.when(pl.program_id(0) == 0)
    def _():
        zero_ref[...] = jnp.zeros_like(zero_ref)

        def put(r, c):
            _row_copy(zero_ref, 0, xs_ref, r, sem).start()
            return c

        def got(r, c):
            _row_copy(zero_ref, 0, xs_ref, 0, sem).wait()
            return c

        def over_pad_rows(fn):
            def per_expert(e, carry):
                return lax.fori_loop(pad_lo_ref[e], pad_hi_ref[e], fn, carry)

            lax.fori_loop(0, N_EXPERTS, per_expert, 0)

        over_pad_rows(put)
        over_pad_rows(got)

    def issue(r, carry):
        for k in range(TOP_K):
            _row_copy(h_ref, r, xs_ref, dest_ref[r * TOP_K + k], sem).start(priority=k % 2)
        return carry

    lax.fori_loop(0, tm, issue, 0)
    for k in range(TOP_K):
        pltpu.make_async_copy(h_ref, xs_ref.at[pl.ds(0, tm), :], sem).wait()


def _dispatch(pad_lo, pad_hi, dest_flat, hp, rows):
    n, w = hp.shape
    tm = ROW_TILE
    grid_spec = pltpu.PrefetchScalarGridSpec(
        num_scalar_prefetch=2,
        grid=(n // tm,),
        in_specs=[
            pl.BlockSpec((tm * TOP_K,), lambda i, lo, hi: (i,), memory_space=pltpu.SMEM),
            pl.BlockSpec((tm, w), lambda i, lo, hi: (i, 0)),
        ],
        out_specs=pl.BlockSpec(memory_space=pl.ANY),
        scratch_shapes=[pltpu.VMEM((8, w), F32), pltpu.SemaphoreType.DMA(())],
    )
    return pl.pallas_call(
        _dispatch_kernel,
        out_shape=jax.ShapeDtypeStruct((rows, w), F32),
        grid_spec=grid_spec,
        compiler_params=_cparams(("arbitrary",)),
        name="dispatch",
    )(pad_lo, pad_hi, dest_flat, hp)


def _moe_kernel(ie_ref, ix_ref, ic_ref, iv_ref, ns_ref, xs_ref, w1_ref, b1g_ref, b1l_ref, w2_ref, b2_ref,
                p_ref, ys_ref, xb_ref, gl_ref, w2b_ref):
    del ie_ref, ix_ref
    item = pl.program_id(0)
    nj, fc, _ = w2b_ref.shape
    chunk = ic_ref[item]

    def ffn(j, rows):
        xt = xb_ref[:rows, :]
        ug = jnp.dot(xt, gl_ref[j, :, :fc], preferred_element_type=F32) + b1g_ref[j]
        ul = jnp.dot(xt, gl_ref[j, :, fc:], preferred_element_type=F32) + b1l_ref[j]
        glu = jnp.minimum(ug, SWIGLU_LIMIT)
        lin = jnp.clip(ul, -SWIGLU_LIMIT, SWIGLU_LIMIT)
        act = glu * jax.nn.sigmoid(SWIGLU_ALPHA * glu) * (lin + 1.0)
        ys_ref[:rows, :] += jnp.dot(act.astype(BF16), w2b_ref[j], preferred_element_type=F32)

    def for_used_rows(fn):
        for n_sub in range(1, MOE_TILE // MOE_SUB + 1):
            pl.when(ns_ref[item] == n_sub)(functools.partial(fn, n_sub * MOE_SUB))

    @pl.when(iv_ref[item] == 1)
    def _():
        @pl.when((chunk == 0) | (chunk == nj))
        def _():
            def load_rows(rows):
                xb_ref[:rows, :] = xs_ref[:rows, :].astype(BF16)

            for_used_rows(load_rows)
            ys_ref[...] = jnp.broadcast_to(b2_ref[...], ys_ref.shape)

        @pl.when(chunk < nj)
        def _():
            p = p_ref[...]
            for blk in range(2 * fc // 256):
                wb = w1_ref[:, blk * 256:(blk + 1) * 256].astype(BF16)
                split = jnp.dot(wb, p, preferred_element_type=F32).astype(BF16)
                gl_ref[chunk, :, blk * LANES:(blk + 1) * LANES] = split[:, :LANES]
                gl_ref[chunk, :, fc + blk * LANES:fc + (blk + 1) * LANES] = split[:, LANES:]
            w2b_ref[chunk] = w2_ref[...].astype(BF16)
            for_used_rows(functools.partial(ffn, chunk))

        @pl.when(chunk == nj)
        def _():
            def all_chunks(rows):
                def body(j, carry):
                    ffn(j, rows)
                    return carry

                lax.fori_loop(0, nj, body, 0, unroll=4)

            for_used_rows(all_chunks)


def _moe_items(tile_e, tile_v, tile_f, tile_ns, nj):
    n_tiles = tile_v.shape[0]
    n_items = n_tiles + N_EXPERTS * (nj - 1)
    cost = jnp.where(tile_f == 1, nj, 1) * tile_v
    first_item = jnp.cumsum(cost) - cost
    total = jnp.sum(cost)
    last_tile = jnp.maximum(jnp.sum(tile_v) - 1, 0)
    item = jnp.arange(n_items, dtype=I32)
    valid = item < total
    tile = jnp.where(valid, jnp.sum(first_item[None, :] <= item[:, None], axis=1) - 1, last_tile).astype(I32)
    chunk = jnp.where(valid & (tile_f[tile] == 1), item - first_item[tile], nj).astype(I32)
    return tile_e[tile], tile, chunk, valid.astype(I32), (tile_ns[tile] * valid).astype(I32)


def _moe(item_e, item_x, item_c, item_v, item_ns, xs, w1, b1g, b1l, w2, b2, perm):
    rows, d = xs.shape
    f = w1.shape[2] // 2
    fc = MOE_FC
    nj = f // fc

    def wj(q, ic):
        return jnp.minimum(ic[q], nj - 1)

    grid_spec = pltpu.PrefetchScalarGridSpec(
        num_scalar_prefetch=5,
        grid=(item_e.shape[0],),
        in_specs=[
            pl.BlockSpec((MOE_TILE, d), lambda q, ie, ix, ic, iv, ns: (ix[q], 0)),
            pl.BlockSpec((None, d, 2 * fc), lambda q, ie, ix, ic, iv, ns: (ie[q], 0, wj(q, ic))),
            pl.BlockSpec((None, nj, 1, fc), lambda q, ie, ix, ic, iv, ns: (ie[q], 0, 0, 0)),
            pl.BlockSpec((None, nj, 1, fc), lambda q, ie, ix, ic, iv, ns: (ie[q], 0, 0, 0)),
            pl.BlockSpec((None, fc, d), lambda q, ie, ix, ic, iv, ns: (ie[q], wj(q, ic), 0)),
            pl.BlockSpec((None, 1, d), lambda q, ie, ix, ic, iv, ns: (ie[q], 0, 0)),
            pl.BlockSpec((256, 256), lambda q, ie, ix, ic, iv, ns: (0, 0)),
        ],
        out_specs=pl.BlockSpec((MOE_TILE, d), lambda q, ie, ix, ic, iv, ns: (ix[q], 0)),
        scratch_shapes=[
            pltpu.VMEM((MOE_TILE, d), BF16),
            pltpu.VMEM((nj, d, 2 * fc), BF16),
            pltpu.VMEM((nj, fc, d), BF16),
        ],
    )
    return pl.pallas_call(
        _moe_kernel,
        out_shape=jax.ShapeDtypeStruct((rows, d), F32),
        grid_spec=grid_spec,
        compiler_params=_cparams(("arbitrary",)),
        name="moe",
    )(item_e, item_x, item_c, item_v, item_ns, xs, w1, b1g, b1l, w2, b2, perm)


def _combine_kernel(dest_ref, next_ref, w_ref, x1_ref, g2_ref, gf_ref, ys_ref, o_ref, buf_ref, sem):
    i = pl.program_id(0)
    tm = x1_ref.shape[0]

    def gather(idx_ref, s):
        def issue(r, carry):
            for k in range(TOP_K):
                _row_copy(ys_ref, idx_ref[r * TOP_K + k], buf_ref.at[s], k * tm + r, sem.at[s]).start(priority=k % 2)
            return carry

        lax.fori_loop(0, tm, issue, 0)

    @pl.when(i == 0)
    def _():
        gather(dest_ref, 0)

    def reduce_slot(s):
        @pl.when(i + 1 < pl.num_programs(0))
        def _():
            gather(next_ref, 1 - s)

        pltpu.make_async_copy(ys_ref.at[pl.ds(0, TOP_K * tm), :], buf_ref.at[s], sem.at[s]).wait()
        w = w_ref[...]
        moe = w[:, 0:1] * buf_ref[s, :tm, :]
        for k in range(1, TOP_K):
            moe = moe + w[:, k:k + 1] * buf_ref[s, k * tm:(k + 1) * tm, :]
        x = x1_ref[...] + g2_ref[0] * moe
        o_ref[...] = _rms(x) * gf_ref[...]

    for s in range(2):
        pl.when(i % 2 == s)(functools.partial(reduce_slot, s))


def _combine(dest_flat, weights, x1, g2, g_final, ys, seq):
    n, d = x1.shape
    tm = COMBINE_TILE
    per_b = seq // tm
    steps = n // tm
    return pl.pallas_call(
        _combine_kernel,
        out_shape=jax.ShapeDtypeStruct((n, d), F32),
        grid=(steps,),
        in_specs=[
            pl.BlockSpec((tm * TOP_K,), lambda i: (i,), memory_space=pltpu.SMEM),
            pl.BlockSpec((tm * TOP_K,), lambda i: (jnp.minimum(i + 1, steps - 1),), memory_space=pltpu.SMEM),
            pl.BlockSpec((tm, TOP_K), lambda i: (i, 0)),
            pl.BlockSpec((tm, d), lambda i: (i, 0)),
            pl.BlockSpec((1, 1, d), lambda i: (i // per_b, 0, 0)),
            _resident((1, d)),
            pl.BlockSpec(memory_space=pl.ANY),
        ],
        out_specs=pl.BlockSpec((tm, d), lambda i: (i, 0)),
        scratch_shapes=[pltpu.VMEM((2, TOP_K * tm, d), F32), pltpu.SemaphoreType.DMA((2,))],
        compiler_params=_cparams(("arbitrary",)),
        name="combine",
    )(dest_flat, dest_flat, weights, x1, g2, g_final, ys)


def _rope_lane_tables():
    inv_freq = 1.0 / (ROPE_THETA ** (jnp.arange(ROPE_HALF, dtype=F32) * 2.0 / ROPE_DIM))
    z = jnp.zeros((ROPE_HALF,), F32)
    o = jnp.ones((ROPE_HALF,), F32)
    invf = jnp.concatenate([inv_freq, z, inv_freq, z])[None, :]
    cmask = jnp.concatenate([o, z, o, z])[None, :]
    smask = jnp.concatenate([-o, z, o, z])[None, :]
    return invf, cmask, smask


def _pad_rope_cols(w):
    z = jnp.zeros(w.shape[:-1] + (ROPE_HALF,), w.dtype)
    return jnp.concatenate([w[..., :ROPE_HALF], z, w[..., ROPE_HALF:], z], axis=-1)


def kernel(x, c, positions, g_attn, w_mod, b_mod, w_in, g_q_lat, w_q_up, g_kv_lat, w_kv_up, w_out, g_ffn,
           w_router, b_router, w1, b1, w2, b2, g_final):
    batch, seq, d = x.shape
    n = batch * seq
    depth = w_mod.shape[0]
    assert depth == 1 and seq % ATTN_TILE == 0 and seq % ROW_TILE == 0 and d == SB_WIDTH + MLA_WIDTH
    x2 = x.reshape(n, d)

    c_pad = jnp.zeros((8, d), F32).at[:batch].set(c)
    mod = _mod(c_pad, w_mod[0], b_mod[0][None, :])[:batch]
    shift1, scale1, gate1, shift2, scale2, gate2 = [m[:, None, :] for m in jnp.split(mod, 6, axis=-1)]

    wi = w_in[0]
    o = 3 * SB_WIDTH + 2 * LORA
    w_all = wi[:, :o].astype(BF16)
    w_kr = _pad_rope_cols(wi[:, o:]).astype(BF16)
    wq = w_q_up[0].reshape(LORA, MLA_HEADS, MLA_QK_DIM)
    wq = jnp.concatenate([wq[..., :HEAD_DIM], _pad_rope_cols(wq[..., HEAD_DIM:])], axis=-1)
    wq = wq.reshape(LORA, MLA_HEADS * 2 * HEAD_DIM).astype(BF16)
    wkv = w_kv_up[0].astype(BF16)
    invf, cmask, smask = _rope_lane_tables()

    sbq, sbk, sbv, mq, mkv = _inproj(
        x2, 1.0 + scale1, shift1, g_attn, positions.reshape(n, 1), invf, cmask, smask, w_all, w_kr,
        g_q_lat, wq, g_kv_lat, wkv, seq)

    sb_out = _sb_attention(sbq, sbk, sbv, batch, seq)
    mla_out = _mla_attention(mq, mkv, batch, seq)

    x1, hp, info_i, info_w, counts = _outproj(
        sb_out, mla_out, x2, gate1, 1.0 + scale2, shift2, g_ffn, w_out[0].astype(BF16),
        w_router[0].T, b_router.reshape(N_EXPERTS, 1), seq)

    counts = counts[:, 0].astype(I32)
    padded = (counts + MOE_TILE - 1) // MOE_TILE * MOE_TILE
    pend = jnp.cumsum(padded)
    pstart = pend - padded
    idx = info_i[:TOP_K].T
    dest = (pstart[idx] + info_i[TOP_K:].T).reshape(n * TOP_K)
    rows = n * TOP_K + N_EXPERTS * MOE_TILE
    n_tiles = rows // MOE_TILE
    tile_row0 = jnp.arange(n_tiles, dtype=I32) * MOE_TILE
    tile_v = (tile_row0 < pend[-1]).astype(I32)
    last = jnp.maximum(pend[-1] // MOE_TILE - 1, 0)
    tile_x = jnp.minimum(jnp.arange(n_tiles, dtype=I32), last)
    tile_e = jnp.minimum(jnp.sum(pend[None, :] <= (tile_x * MOE_TILE)[:, None], axis=1), N_EXPERTS - 1).astype(I32)
    tile_f = (tile_x * MOE_TILE == pstart[tile_e]).astype(I32) * tile_v
    used = counts[tile_e] - (tile_x * MOE_TILE - pstart[tile_e])
    tile_ns = jnp.clip((used + MOE_SUB - 1) // MOE_SUB, 0, MOE_TILE // MOE_SUB) * tile_v

    xs = _dispatch(pstart + counts, pstart + (counts + MOE_SUB - 1) // MOE_SUB * MOE_SUB, dest, hp, rows)

    src = jnp.arange(256)
    perm = (jnp.arange(256)[None, :] == (src % 2 * LANES + src // 2)[:, None]).astype(BF16)
    f = w1.shape[-1] // 2
    nj = f // MOE_FC
    b1r = b1[0].reshape(N_EXPERTS, nj, 1, MOE_FC, 2)
    items = _moe_items(tile_e, tile_v, tile_f, tile_ns, nj)
    ys = _moe(*items, xs, w1[0], b1r[..., 0], b1r[..., 1], w2[0], b2[0][:, None, :], perm)

    out = _combine(dest, info_w[:TOP_K].T, x1, gate2, g_final[None, :], ys, seq)
    return out.reshape(batch, seq, d)
```

```python
import functools

import jax
import jax.numpy as jnp
from jax import lax
from jax.experimental import pallas as pl
from jax.experimental.pallas import tpu as pltpu

F32 = jnp.float32
BF16 = jnp.bfloat16
I32 = jnp.int32

SB_HEADS = 8
MLA_HEADS = 8
HEAD_DIM = 128
ROPE_DIM = 64
ROPE_HALF = ROPE_DIM // 2
MLA_QK_DIM = HEAD_DIM + ROPE_DIM
LORA = 512
ROPE_THETA = 10000.0
N_EXPERTS = 32
TOP_K = 4
SWIGLU_ALPHA = 1.702
SWIGLU_LIMIT = 7.0
NORM_EPS = 1e-6
SB_WIDTH = SB_HEADS * HEAD_DIM
MLA_WIDTH = MLA_HEADS * HEAD_DIM

LANES = 128
VMEM_LIMIT_BYTES = 56 * 1024 * 1024
MLA_VMEM_LIMIT_BYTES = 60 * 1024 * 1024

ROW_TILE = 256
OUTPROJ_TILE = 512
ATTN_TILE = 256
MLA_Q_TILE = 512
MLA_K_TILE = 512
MLA_BOUND_SLACK = 1.01
MLA_MIN_ROW_SUM = 1e-30
ATTN_HEADS_PER_STEP = 4
MLA_HEADS_PER_STEP = 8
MLA_KV_STRIDE = 3 * HEAD_DIM
SB_DEAD_LOG = -104.0
MOE_TILE = 512
MOE_SUB = 256
MOE_FC = 256
COMBINE_TILE = 256
NEG_BIG = -1e30


def _cparams(sem, vmem_limit_bytes=VMEM_LIMIT_BYTES):
    return pltpu.CompilerParams(dimension_semantics=sem, vmem_limit_bytes=vmem_limit_bytes)


def _resident(shape):
    nd = len(shape)
    return pl.BlockSpec(shape, lambda *_: (0,) * nd, pipeline_mode=pl.Buffered(1))


def _rms(v):
    return v * lax.rsqrt(jnp.mean(v * v, axis=-1, keepdims=True) + NORM_EPS)


def _mod_kernel(c_ref, w_ref, b_ref, o_ref):
    c = c_ref[...]
    a = (c * jax.nn.sigmoid(c)).astype(BF16)
    o_ref[...] = jnp.dot(a, w_ref[...].astype(BF16), preferred_element_type=F32) + b_ref[...]


def _mod(c_pad, w_mod, b_mod):
    rows, d = c_pad.shape
    n_out = w_mod.shape[1]
    tn = 1024
    return pl.pallas_call(
        _mod_kernel,
        out_shape=jax.ShapeDtypeStruct((rows, n_out), F32),
        grid=(n_out // tn,),
        in_specs=[
            pl.BlockSpec((rows, d), lambda j: (0, 0)),
            pl.BlockSpec((d, tn), lambda j: (0, j)),
            pl.BlockSpec((1, tn), lambda j: (0, j)),
        ],
        out_specs=pl.BlockSpec((rows, tn), lambda j: (0, j)),
        compiler_params=_cparams(("arbitrary",)),
        name="mod",
    )(c_pad, w_mod, b_mod)


def _inproj_kernel(x_ref, sc_ref, sh_ref, g_ref, pos_ref, invf_ref, cm_ref, sm_ref, w_ref, wkr_ref,
                   gq_ref, wq_ref, gkv_ref, wkv_ref,
                   sbq_ref, sbk_ref, sbv_ref, mq_ref, mkv_ref, ksq_ref):
    h = (_rms(x_ref[...]) * g_ref[...] * sc_ref[0] + sh_ref[0]).astype(BF16)
    proj = jnp.dot(h, w_ref[...], preferred_element_type=F32)

    sbq_ref[...] = (proj[:, :SB_WIDTH] * (HEAD_DIM ** -0.5)).astype(BF16)
    sbk_ref[...] = proj[:, SB_WIDTH:2 * SB_WIDTH].astype(BF16)
    sbv_ref[...] = proj[:, 2 * SB_WIDTH:3 * SB_WIDTH].astype(BF16)

    pos = jnp.broadcast_to(pos_ref[...].astype(F32), (LANES, x_ref.shape[0]))
    ang = pos.T * invf_ref[...]
    cosm = jnp.cos(ang) * cm_ref[...]
    sinm = jnp.sin(ang) * sm_ref[...]

    def rope(v):
        return v * cosm + pltpu.roll(v, LANES // 2, 1) * sinm

    o = 3 * SB_WIDTH
    q_lat = (_rms(proj[:, o:o + LORA]) * gq_ref[...]).astype(BF16)
    q = jnp.dot(q_lat, wq_ref[...], preferred_element_type=F32) * (MLA_QK_DIM ** -0.5)
    for hd in range(MLA_HEADS):
        c0 = hd * 2 * HEAD_DIM
        mq_ref[:, c0:c0 + HEAD_DIM] = q[:, c0:c0 + HEAD_DIM].astype(BF16)
        mq_ref[:, c0 + HEAD_DIM:c0 + 2 * HEAD_DIM] = rope(q[:, c0 + HEAD_DIM:c0 + 2 * HEAD_DIM]).astype(BF16)

    kv_lat = (_rms(proj[:, o + LORA:o + 2 * LORA]) * gkv_ref[...]).astype(BF16)
    kv = jnp.dot(kv_lat, wkv_ref[...], preferred_element_type=F32).astype(BF16)
    k_rope = rope(jnp.dot(h, wkr_ref[...], preferred_element_type=F32)).astype(BF16)
    kr32 = k_rope.astype(F32)
    rope_sq = jnp.sum(kr32 * kr32, axis=1, keepdims=True)
    key_sq = []
    for hd in range(MLA_HEADS):
        c0 = hd * MLA_KV_STRIDE
        k_nope = kv[:, hd * 2 * HEAD_DIM:hd * 2 * HEAD_DIM + HEAD_DIM]
        mkv_ref[:, c0:c0 + HEAD_DIM] = k_nope
        mkv_ref[:, c0 + HEAD_DIM:c0 + 2 * HEAD_DIM] = k_rope
        mkv_ref[:, c0 + 2 * HEAD_DIM:c0 + 3 * HEAD_DIM] = kv[:, hd * 2 * HEAD_DIM + HEAD_DIM:(hd + 1) * 2 * HEAD_DIM]
        kn32 = k_nope.astype(F32)
        sq = jnp.sum(kn32 * kn32, axis=1, keepdims=True) + rope_sq
        key_sq.append(jnp.broadcast_to(jnp.max(sq, axis=0, keepdims=True), (1, LANES)))
    ksq_ref[0] = jnp.concatenate(key_sq, axis=0)


def _inproj(x2, sc1, sh1, g_attn, pos, invf, cmask, smask, w_all, w_kr, g_q, wq, g_kv, wkv, seq):
    n, d = x2.shape
    tm = ROW_TILE
    wcols = w_all.shape[1]
    per_b = seq // tm
    row = lambda i: (i, 0)
    bat = lambda i: (i // per_b, 0, 0)
    outs = [
        jax.ShapeDtypeStruct((n, SB_WIDTH), BF16),
        jax.ShapeDtypeStruct((n, SB_WIDTH), BF16),
        jax.ShapeDtypeStruct((n, SB_WIDTH), BF16),
        jax.ShapeDtypeStruct((n, 2 * MLA_WIDTH), BF16),
        jax.ShapeDtypeStruct((n, MLA_HEADS * MLA_KV_STRIDE), BF16),
        jax.ShapeDtypeStruct((n // tm, MLA_HEADS, LANES), F32),
    ]
    return pl.pallas_call(
        _inproj_kernel,
        out_shape=outs,
        grid=(n // tm,),
        in_specs=[
            pl.BlockSpec((tm, d), row),
            pl.BlockSpec((1, 1, d), bat),
            pl.BlockSpec((1, 1, d), bat),
            _resident((1, d)),
            pl.BlockSpec((1, tm), lambda i: (0, i)),
            _resident((1, LANES)),
            _resident((1, LANES)),
            _resident((1, LANES)),
            _resident((d, wcols)),
            _resident((d, LANES)),
            _resident((1, LORA)),
            _resident((LORA, 2 * MLA_WIDTH)),
            _resident((1, LORA)),
            _resident((LORA, 2 * MLA_WIDTH)),
        ],
        out_specs=[
            pl.BlockSpec((tm, SB_WIDTH), row),
            pl.BlockSpec((tm, SB_WIDTH), row),
            pl.BlockSpec((tm, SB_WIDTH), row),
            pl.BlockSpec((tm, 2 * MLA_WIDTH), row),
            pl.BlockSpec((tm, MLA_HEADS * MLA_KV_STRIDE), row),
            pl.BlockSpec((1, MLA_HEADS, LANES), lambda i: (i, 0, 0)),
        ],
        compiler_params=_cparams(("arbitrary",)),
        name="inproj",
    )(x2, sc1, sh1, g_attn, pos, invf, cmask, smask, w_all, w_kr, g_q, wq, g_kv, wkv)


_NT = (((1,), (1,)), ((), ()))


def _sb_kernel(q_ref, k_ref, v_ref, o_ref, *, tile, heads):
    i = pl.program_id(2)
    row = lax.broadcasted_iota(I32, (tile, tile), 0)
    col = lax.broadcasted_iota(I32, (tile, tile), 1)
    suffix = jnp.where(row >= col, 1.0, 0.0).astype(BF16)
    strict = col < row

    def head_step(h, kt, c, acc, diag):
        hs = slice(h * HEAD_DIM, (h + 1) * HEAD_DIM)
        k0 = pl.multiple_of(kt * tile, tile)
        k = k_ref[pl.ds(k0, tile), hs]
        v = v_ref[pl.ds(k0, tile), hs]
        z = lax.dot_general(q_ref[:, hs], k, _NT, preferred_element_type=F32)
        log_beta = jnp.minimum(z, 0.0) - jnp.log(1.0 + jnp.exp(-jnp.abs(z)))
        log_keep = log_beta - z
        if diag:
            log_keep = jnp.where(strict, log_keep, 0.0)
        incl = jnp.dot(log_keep.astype(BF16), suffix, preferred_element_type=F32)
        a = jnp.exp(incl + z + c)
        if diag:
            a = jnp.where(strict, a, 0.0)
        acc = acc + jnp.dot(a.astype(BF16), v, preferred_element_type=F32)
        return c + incl[:, 0:1], acc

    def step(kt, carry, diag):
        return tuple(head_step(h, kt, *carry[h], diag) for h in range(heads))

    def live(carry):
        c_max = carry[0][0]
        for h in range(1, heads):
            c_max = jnp.maximum(c_max, carry[h][0])
        return (jnp.max(c_max) > SB_DEAD_LOG).astype(I32)

    zero = (jnp.zeros((tile, 1), F32), jnp.zeros((tile, HEAD_DIM), F32))
    carry = step(i, (zero,) * heads, True)

    def body(state):
        kt, _, cr = state
        cr = step(kt, cr, False)
        return kt - 1, live(cr), cr

    _, _, carry = lax.while_loop(lambda st: (st[0] >= 0) & (st[1] > 0), body, (i - 1, live(carry), carry))
    for h in range(heads):
        o_ref[:, h * HEAD_DIM:(h + 1) * HEAD_DIM] = carry[h][1].astype(BF16)


def _sb_attention(sbq, sbk, sbv, batch, seq):
    n = sbq.shape[0]
    t = ATTN_TILE
    hb = ATTN_HEADS_PER_STEP
    nq = seq // t
    qmap = lambda b, h, i: (b * nq + i, h)
    kmap = lambda b, h, i: (b, h)
    return pl.pallas_call(
        functools.partial(_sb_kernel, tile=t, heads=hb),
        out_shape=jax.ShapeDtypeStruct((n, SB_WIDTH), BF16),
        grid=(batch, SB_HEADS // hb, nq),
        in_specs=[
            pl.BlockSpec((t, hb * HEAD_DIM), qmap),
            pl.BlockSpec((seq, hb * HEAD_DIM), kmap),
            pl.BlockSpec((seq, hb * HEAD_DIM), kmap),
        ],
        out_specs=pl.BlockSpec((t, hb * HEAD_DIM), qmap),
        compiler_params=_cparams(("arbitrary", "arbitrary", "arbitrary")),
        name="sb_attn",
    )(sbq, sbk, sbv)


def _mla_kernel(q_ref, kv_ref, kn_ref, o_ref, *, tq, tk, heads):
    i = pl.program_id(2)
    row = lax.broadcasted_iota(I32, (tq, tk), 0)
    col = lax.broadcasted_iota(I32, (tq, tk), 1)
    qk = 2 * HEAD_DIM
    per_q = tq // tk

    def scores(h, kt, diag):
        k0 = pl.multiple_of(kt * tk, tk)
        k = kv_ref[pl.ds(k0, tk), h * MLA_KV_STRIDE:h * MLA_KV_STRIDE + qk]
        s = lax.dot_general(q_ref[:, h * qk:(h + 1) * qk], k, _NT, preferred_element_type=F32)
        if diag:
            s = jnp.where(col + kt * tk <= row + i * tq, s, NEG_BIG)
        return s

    def lane_fold(x, op):
        out = x[:, :LANES]
        for cblk in range(1, tk // LANES):
            out = op(out, x[:, cblk * LANES:(cblk + 1) * LANES])
        return out

    def over_keys(step, carry):
        for r in range(per_q):
            carry = step(i * per_q + r, carry, True)
        return lax.fori_loop(0, i * per_q, lambda s, cr: step(s, cr, False), carry)

    def softmax_pv(shift):
        def acc_step(kt, carry, diag):
            out = []
            for h in range(heads):
                l, acc = carry[h]
                k0 = pl.multiple_of(kt * tk, tk)
                v = kv_ref[pl.ds(k0, tk), h * MLA_KV_STRIDE + qk:(h + 1) * MLA_KV_STRIDE]
                s = scores(h, kt, diag)
                p = jnp.concatenate([jnp.exp(s[:, c * LANES:(c + 1) * LANES] - shift[h])
                                     for c in range(tk // LANES)], axis=1)
                out.append((l + lane_fold(p, jnp.add),
                            acc + jnp.dot(p.astype(BF16), v, preferred_element_type=F32)))
            return tuple(out)

        return over_keys(acc_step, ((jnp.zeros((tq, LANES), F32), jnp.zeros((tq, HEAD_DIM), F32)),) * heads)

    def row_sums(carry):
        return tuple(jnp.sum(carry[h][0], axis=1, keepdims=True) for h in range(heads))

    def write(carry, sums):
        for h in range(heads):
            o_ref[:, h * HEAD_DIM:(h + 1) * HEAD_DIM] = (carry[h][1] / sums[h]).astype(BF16)

    head0 = pl.program_id(1) * heads
    bound = []
    for h in range(heads):
        qh = q_ref[:, h * qk:(h + 1) * qk].astype(F32)
        q_norm = jnp.sqrt(jnp.sum(qh * qh, axis=1, keepdims=True))
        bound.append(q_norm * kn_ref[0, pl.ds(head0 + h, 1), :] * MLA_BOUND_SLACK)
    fast = softmax_pv(tuple(bound))
    fast_sums = row_sums(fast)
    smallest = fast_sums[0]
    for h in range(1, heads):
        smallest = jnp.minimum(smallest, fast_sums[h])
    in_range = jnp.min(smallest) > MLA_MIN_ROW_SUM
    write(fast, fast_sums)

    @pl.when(jnp.logical_not(in_range))
    def _():
        def max_step(kt, ms, diag):
            return tuple(jnp.maximum(ms[h], lane_fold(scores(h, kt, diag), jnp.maximum)) for h in range(heads))

        ms = over_keys(max_step, (jnp.full((tq, LANES), NEG_BIG, F32),) * heads)
        row_max = tuple(jnp.broadcast_to(jnp.max(ms[h], axis=1, keepdims=True), (tq, LANES))
                        for h in range(heads))
        exact = softmax_pv(row_max)
        write(exact, row_sums(exact))


def _mla_attention(mq, mkv, k_norm, batch, seq):
    n = mq.shape[0]
    t = MLA_Q_TILE
    hb = MLA_HEADS_PER_STEP
    nq = seq // t
    return pl.pallas_call(
        functools.partial(_mla_kernel, tq=t, tk=MLA_K_TILE, heads=hb),
        out_shape=jax.ShapeDtypeStruct((n, MLA_WIDTH), BF16),
        grid=(batch, MLA_HEADS // hb, nq),
        in_specs=[
            pl.BlockSpec((t, hb * 2 * HEAD_DIM), lambda b, h, i: (b * nq + i, h)),
            pl.BlockSpec((seq, hb * MLA_KV_STRIDE), lambda b, h, i: (b, h), pipeline_mode=pl.Buffered(1)),
            pl.BlockSpec((1, MLA_HEADS, LANES), lambda b, h, i: (b, 0, 0)),
        ],
        out_specs=pl.BlockSpec((t, hb * HEAD_DIM), lambda b, h, i: (b * nq + i, h)),
        compiler_params=_cparams(("arbitrary", "arbitrary", "arbitrary"), MLA_VMEM_LIMIT_BYTES),
        name="mla_attn",
    )(mq, mkv, k_norm)


def _outproj_kernel(sb_ref, mla_ref, x_ref, g1_ref, sc_ref, sh_ref, g_ref, wo_ref, wr_ref, br_ref,
                    x1_ref, hp_ref, ii_ref, iw_ref, cnt_ref, run_ref):
    step = pl.program_id(0)
    tm, d = x_ref.shape

    @pl.when(step == 0)
    def _():
        run_ref[...] = jnp.zeros_like(run_ref)

    o = (jnp.dot(sb_ref[...], wo_ref[:SB_WIDTH, :], preferred_element_type=F32)
         + jnp.dot(mla_ref[...], wo_ref[SB_WIDTH:, :], preferred_element_type=F32))
    x1 = x_ref[...] + g1_ref[0] * o
    x1_ref[...] = x1
    h2 = _rms(x1) * g_ref[...] * sc_ref[0] + sh_ref[0]
    hb = h2.astype(BF16)
    hb32 = hb.astype(F32)
    hp_ref[...] = hb32

    h_lo = (h2 - hb32).astype(BF16)
    w = wr_ref[...]
    w_hi = w.astype(BF16)
    w_lo = (w - w_hi.astype(F32)).astype(BF16)
    logits = (lax.dot_general(w_hi, hb, _NT, preferred_element_type=F32)
              + lax.dot_general(w_hi, h_lo, _NT, preferred_element_type=F32)
              + lax.dot_general(w_lo, hb, _NT, preferred_element_type=F32)) + br_ref[...]

    expert = lax.broadcasted_iota(I32, (N_EXPERTS, tm), 0)
    work = logits
    vals, idxs, hots = [], [], []
    for _ in range(TOP_K):
        m = jnp.max(work, axis=0, keepdims=True)
        idx = jnp.min(jnp.where(work == m, expert, N_EXPERTS), axis=0, keepdims=True)
        hot = expert == idx
        work = jnp.where(hot, -jnp.inf, work)
        vals.append(m)
        idxs.append(idx)
        hots.append(hot)
    exps = [jnp.exp(v - vals[0]) for v in vals]
    den = exps[0] + exps[1] + exps[2] + exps[3]
    sel = jnp.where(hots[0] | hots[1] | hots[2] | hots[3], 1.0, 0.0)

    r = lax.broadcasted_iota(I32, (tm, tm), 0)
    c = lax.broadcasted_iota(I32, (tm, tm), 1)
    earlier = jnp.where(r < c, 1.0, 0.0).astype(BF16)
    rank = jnp.dot(sel.astype(BF16), earlier, preferred_element_type=F32) + run_ref[...]
    run_ref[...] = run_ref[...] + jnp.sum(sel, axis=1, keepdims=True)
    cnt_ref[...] = run_ref[...]

    field = lax.broadcasted_iota(I32, (2 * TOP_K, tm), 0)
    info_i = jnp.zeros((2 * TOP_K, tm), I32)
    info_w = jnp.zeros((2 * TOP_K, tm), F32)
    for k in range(TOP_K):
        rank_k = jnp.sum(jnp.where(hots[k], rank, 0.0), axis=0, keepdims=True).astype(I32)
        info_i = jnp.where(field == k, idxs[k], info_i)
        info_i = jnp.where(field == TOP_K + k, rank_k, info_i)
        info_w = jnp.where(field == k, exps[k] / den, info_w)
    ii_ref[...] = info_i
    iw_ref[...] = info_w


def _outproj(sb, mla, x2, g1, sc2, sh2, g_ffn, wo, w_router, b_router, seq):
    n, d = x2.shape
    tm = OUTPROJ_TILE
    per_b = seq // tm
    row = lambda i: (i, 0)
    bat = lambda i: (i // per_b, 0, 0)
    outs = [
        jax.ShapeDtypeStruct((n, d), F32),
        jax.ShapeDtypeStruct((n, d), F32),
        jax.ShapeDtypeStruct((2 * TOP_K, n), I32),
        jax.ShapeDtypeStruct((2 * TOP_K, n), F32),
        jax.ShapeDtypeStruct((N_EXPERTS, 1), F32),
    ]
    return pl.pallas_call(
        _outproj_kernel,
        out_shape=outs,
        grid=(n // tm,),
        in_specs=[
            pl.BlockSpec((tm, SB_WIDTH), row),
            pl.BlockSpec((tm, MLA_WIDTH), row),
            pl.BlockSpec((tm, d), row),
            pl.BlockSpec((1, 1, d), bat),
            pl.BlockSpec((1, 1, d), bat),
            pl.BlockSpec((1, 1, d), bat),
            _resident((1, d)),
            _resident((SB_WIDTH + MLA_WIDTH, d)),
            _resident((N_EXPERTS, d)),
            _resident((N_EXPERTS, 1)),
        ],
        out_specs=[
            pl.BlockSpec((tm, d), row),
            pl.BlockSpec((tm, d), row),
            pl.BlockSpec((2 * TOP_K, tm), lambda i: (0, i)),
            pl.BlockSpec((2 * TOP_K, tm), lambda i: (0, i)),
            pl.BlockSpec((N_EXPERTS, 1), lambda i: (0, 0)),
        ],
        scratch_shapes=[pltpu.VMEM((N_EXPERTS, 1), F32)],
        compiler_params=_cparams(("arbitrary",)),
        name="outproj",
    )(sb, mla, x2, g1, sc2, sh2, g_ffn, wo, w_router, b_router)


def _row_copy(src_ref, src_row, dst_ref, dst_row, sem):
    return pltpu.make_async_copy(src_ref.at[pl.ds(src_row, 1), :], dst_ref.at[pl.ds(dst_row, 1), :], sem)


def _dispatch_kernel(pad_lo_ref, pad_hi_ref, dest_ref, h_ref, xs_ref, zero_ref, sem):
    tm = h_ref.shape[0]

    @pl.when(pl.program_id(0) == 0)
    def _():
        zero_ref[...] = jnp.zeros_like(zero_ref)

        def put(r, c):
            _row_copy(zero_ref, 0, xs_ref, r, sem).start()
            return c

        def got(r, c):
            _row_copy(zero_ref, 0, xs_ref, 0, sem).wait()
            return c

        def over_pad_rows(fn):
            def per_expert(e, carry):
                return lax.fori_loop(pad_lo_ref[e], pad_hi_ref[e], fn, carry)

            lax.fori_loop(0, N_EXPERTS, per_expert, 0)

        over_pad_rows(put)
        over_pad_rows(got)

    def issue(r, carry):
        for k in range(TOP_K):
            _row_copy(h_ref, r, xs_ref, dest_ref[r * TOP_K + k], sem).start(priority=k % 2)
        return carry

    lax.fori_loop(0, tm, issue, 0)
    for k in range(TOP_K):
        pltpu.make_async_copy(h_ref, xs_ref.at[pl.ds(0, tm), :], sem).wait()


def _dispatch(pad_lo, pad_hi, dest_flat, hp, rows):
    n, w = hp.shape
    tm = ROW_TILE
    grid_spec = pltpu.PrefetchScalarGridSpec(
        num_scalar_prefetch=2,
        grid=(n // tm,),
        in_specs=[
            pl.BlockSpec((tm * TOP_K,), lambda i, lo, hi: (i,), memory_space=pltpu.SMEM),
            pl.BlockSpec((tm, w), lambda i, lo, hi: (i, 0)),
        ],
        out_specs=pl.BlockSpec(memory_space=pl.ANY),
        scratch_shapes=[pltpu.VMEM((8, w), F32), pltpu.SemaphoreType.DMA(())],
    )
    return pl.pallas_call(
        _dispatch_kernel,
        out_shape=jax.ShapeDtypeStruct((rows, w), F32),
        grid_spec=grid_spec,
        compiler_params=_cparams(("arbitrary",)),
        name="dispatch",
    )(pad_lo, pad_hi, dest_flat, hp)


def _moe_kernel(ie_ref, ix_ref, ic_ref, iv_ref, ns_ref, xs_ref, w1_ref, b1g_ref, b1l_ref, w2_ref, b2_ref,
                p_ref, ys_ref, xb_ref, gl_ref, w2b_ref):
    del ie_ref, ix_ref
    item = pl.program_id(0)
    nj, fc, _ = w2b_ref.shape
    chunk = ic_ref[item]

    def ffn(j, rows):
        xt = xb_ref[:rows, :]
        ug = jnp.dot(xt, gl_ref[j, :, :fc], preferred_element_type=F32) + b1g_ref[j]
        ul = jnp.dot(xt, gl_ref[j, :, fc:], preferred_element_type=F32) + b1l_ref[j]
        glu = jnp.minimum(ug, SWIGLU_LIMIT)
        lin = jnp.clip(ul, -SWIGLU_LIMIT, SWIGLU_LIMIT)
        act = glu * jax.nn.sigmoid(SWIGLU_ALPHA * glu) * (lin + 1.0)
        ys_ref[:rows, :] += jnp.dot(act.astype(BF16), w2b_ref[j], preferred_element_type=F32)

    def for_used_rows(fn):
        for n_sub in range(1, MOE_TILE // MOE_SUB + 1):
            pl.when(ns_ref[item] == n_sub)(functools.partial(fn, n_sub * MOE_SUB))

    @pl.when(iv_ref[item] == 1)
    def _():
        @pl.when((chunk == 0) | (chunk == nj))
        def _():
            def load_rows(rows):
                xb_ref[:rows, :] = xs_ref[:rows, :].astype(BF16)

            for_used_rows(load_rows)
            ys_ref[...] = jnp.broadcast_to(b2_ref[...], ys_ref.shape)

        @pl.when(chunk < nj)
        def _():
            p = p_ref[...]
            for blk in range(2 * fc // 256):
                wb = w1_ref[:, blk * 256:(blk + 1) * 256].astype(BF16)
                split = jnp.dot(wb, p, preferred_element_type=F32).astype(BF16)
                gl_ref[chunk, :, blk * LANES:(blk + 1) * LANES] = split[:, :LANES]
                gl_ref[chunk, :, fc + blk * LANES:fc + (blk + 1) * LANES] = split[:, LANES:]
            w2b_ref[chunk] = w2_ref[...].astype(BF16)
            for_used_rows(functools.partial(ffn, chunk))

        @pl.when(chunk == nj)
        def _():
            def all_chunks(rows):
                def body(j, carry):
                    ffn(j, rows)
                    return carry

                lax.fori_loop(0, nj, body, 0, unroll=4)

            for_used_rows(all_chunks)


def _moe_items(tile_e, tile_v, tile_f, tile_ns, nj):
    n_tiles = tile_v.shape[0]
    n_items = n_tiles + N_EXPERTS * (nj - 1)
    cost = jnp.where(tile_f == 1, nj, 1) * tile_v
    first_item = jnp.cumsum(cost) - cost
    total = jnp.sum(cost)
    last_tile = jnp.maximum(jnp.sum(tile_v) - 1, 0)
    item = jnp.arange(n_items, dtype=I32)
    valid = item < total
    tile = jnp.where(valid, jnp.sum(first_item[None, :] <= item[:, None], axis=1) - 1, last_tile).astype(I32)
    chunk = jnp.where(valid & (tile_f[tile] == 1), item - first_item[tile], nj).astype(I32)
    return tile_e[tile], tile, chunk, valid.astype(I32), (tile_ns[tile] * valid).astype(I32)


def _moe(item_e, item_x, item_c, item_v, item_ns, xs, w1, b1g, b1l, w2, b2, perm):
    rows, d = xs.shape
    f = w1.shape[2] // 2
    fc = MOE_FC
    nj = f // fc

    def wj(q, ic):
        return jnp.minimum(ic[q], nj - 1)

    grid_spec = pltpu.PrefetchScalarGridSpec(
        num_scalar_prefetch=5,
        grid=(item_e.shape[0],),
        in_specs=[
            pl.BlockSpec((MOE_TILE, d), lambda q, ie, ix, ic, iv, ns: (ix[q], 0)),
            pl.BlockSpec((None, d, 2 * fc), lambda q, ie, ix, ic, iv, ns: (ie[q], 0, wj(q, ic))),
            pl.BlockSpec((None, nj, 1, fc), lambda q, ie, ix, ic, iv, ns: (ie[q], 0, 0, 0)),
            pl.BlockSpec((None, nj, 1, fc), lambda q, ie, ix, ic, iv, ns: (ie[q], 0, 0, 0)),
            pl.BlockSpec((None, fc, d), lambda q, ie, ix, ic, iv, ns: (ie[q], wj(q, ic), 0)),
            pl.BlockSpec((None, 1, d), lambda q, ie, ix, ic, iv, ns: (ie[q], 0, 0)),
            pl.BlockSpec((256, 256), lambda q, ie, ix, ic, iv, ns: (0, 0)),
        ],
        out_specs=pl.BlockSpec((MOE_TILE, d), lambda q, ie, ix, ic, iv, ns: (ix[q], 0)),
        scratch_shapes=[
            pltpu.VMEM((MOE_TILE, d), BF16),
            pltpu.VMEM((nj, d, 2 * fc), BF16),
            pltpu.VMEM((nj, fc, d), BF16),
        ],
    )
    return pl.pallas_call(
        _moe_kernel,
        out_shape=jax.ShapeDtypeStruct((rows, d), F32),
        grid_spec=grid_spec,
        compiler_params=_cparams(("arbitrary",)),
        name="moe",
    )(item_e, item_x, item_c, item_v, item_ns, xs, w1, b1g, b1l, w2, b2, perm)


def _combine_kernel(dest_ref, next_ref, w_ref, x1_ref, g2_ref, gf_ref, ys_ref, o_ref, buf_ref, sem):
    i = pl.program_id(0)
    tm = x1_ref.shape[0]

    def gather(idx_ref, s):
        def issue(r, carry):
            for k in range(TOP_K):
                _row_copy(ys_ref, idx_ref[r * TOP_K + k], buf_ref.at[s], k * tm + r, sem.at[s]).start(priority=k % 2)
            return carry

        lax.fori_loop(0, tm, issue, 0)

    @pl.when(i == 0)
    def _():
        gather(dest_ref, 0)

    def reduce_slot(s):
        @pl.when(i + 1 < pl.num_programs(0))
        def _():
            gather(next_ref, 1 - s)

        pltpu.make_async_copy(ys_ref.at[pl.ds(0, TOP_K * tm), :], buf_ref.at[s], sem.at[s]).wait()
        w = w_ref[...]
        moe = w[:, 0:1] * buf_ref[s, :tm, :]
        for k in range(1, TOP_K):
            moe = moe + w[:, k:k + 1] * buf_ref[s, k * tm:(k + 1) * tm, :]
        x = x1_ref[...] + g2_ref[0] * moe
        o_ref[...] = _rms(x) * gf_ref[...]

    for s in range(2):
        pl.when(i % 2 == s)(functools.partial(reduce_slot, s))


def _combine(dest_flat, weights, x1, g2, g_final, ys, seq):
    n, d = x1.shape
    tm = COMBINE_TILE
    per_b = seq // tm
    steps = n // tm
    return pl.pallas_call(
        _combine_kernel,
        out_shape=jax.ShapeDtypeStruct((n, d), F32),
        grid=(steps,),
        in_specs=[
            pl.BlockSpec((tm * TOP_K,), lambda i: (i,), memory_space=pltpu.SMEM),
            pl.BlockSpec((tm * TOP_K,), lambda i: (jnp.minimum(i + 1, steps - 1),), memory_space=pltpu.SMEM),
            pl.BlockSpec((tm, TOP_K), lambda i: (i, 0)),
            pl.BlockSpec((tm, d), lambda i: (i, 0)),
            pl.BlockSpec((1, 1, d), lambda i: (i // per_b, 0, 0)),
            _resident((1, d)),
            pl.BlockSpec(memory_space=pl.ANY),
        ],
        out_specs=pl.BlockSpec((tm, d), lambda i: (i, 0)),
        scratch_shapes=[pltpu.VMEM((2, TOP_K * tm, d), F32), pltpu.SemaphoreType.DMA((2,))],
        compiler_params=_cparams(("arbitrary",)),
        name="combine",
    )(dest_flat, dest_flat, weights, x1, g2, g_final, ys)


def _rope_lane_tables():
    inv_freq = 1.0 / (ROPE_THETA ** (jnp.arange(ROPE_HALF, dtype=F32) * 2.0 / ROPE_DIM))
    z = jnp.zeros((ROPE_HALF,), F32)
    o = jnp.ones((ROPE_HALF,), F32)
    invf = jnp.concatenate([inv_freq, z, inv_freq, z])[None, :]
    cmask = jnp.concatenate([o, z, o, z])[None, :]
    smask = jnp.concatenate([-o, z, o, z])[None, :]
    return invf, cmask, smask


def _pad_rope_cols(w):
    z = jnp.zeros(w.shape[:-1] + (ROPE_HALF,), w.dtype)
    return jnp.concatenate([w[..., :ROPE_HALF], z, w[..., ROPE_HALF:], z], axis=-1)


def kernel(x, c, positions, g_attn, w_mod, b_mod, w_in, g_q_lat, w_q_up, g_kv_lat, w_kv_up, w_out, g_ffn,
           w_router, b_router, w1, b1, w2, b2, g_final):
    batch, seq, d = x.shape
    n = batch * seq
    depth = w_mod.shape[0]
    assert depth == 1 and seq % ATTN_TILE == 0 and seq % ROW_TILE == 0 and d == SB_WIDTH + MLA_WIDTH
    x2 = x.reshape(n, d)

    c_pad = jnp.zeros((8, d), F32).at[:batch].set(c)
    mod = _mod(c_pad, w_mod[0], b_mod[0][None, :])[:batch]
    shift1, scale1, gate1, shift2, scale2, gate2 = [m[:, None, :] for m in jnp.split(mod, 6, axis=-1)]

    wi = w_in[0]
    o = 3 * SB_WIDTH + 2 * LORA
    w_all = wi[:, :o].astype(BF16)
    w_kr = _pad_rope_cols(wi[:, o:]).astype(BF16)
    wq = w_q_up[0].reshape(LORA, MLA_HEADS, MLA_QK_DIM)
    wq = jnp.concatenate([wq[..., :HEAD_DIM], _pad_rope_cols(wq[..., HEAD_DIM:])], axis=-1)
    wq = wq.reshape(LORA, MLA_HEADS * 2 * HEAD_DIM).astype(BF16)
    wkv = w_kv_up[0].astype(BF16)
    invf, cmask, smask = _rope_lane_tables()

    sbq, sbk, sbv, mq, mkv, key_sq = _inproj(
        x2, 1.0 + scale1, shift1, g_attn, positions.reshape(1, n), invf, cmask, smask, w_all, w_kr,
        g_q_lat, wq, g_kv_lat, wkv, seq)

    sb_out = _sb_attention(sbq, sbk, sbv, batch, seq)
    k_norm = jnp.sqrt(jnp.max(key_sq.reshape(batch, -1, MLA_HEADS, LANES), axis=1))
    mla_out = _mla_attention(mq, mkv, k_norm, batch, seq)

    x1, hp, info_i, info_w, counts = _outproj(
        sb_out, mla_out, x2, gate1, 1.0 + scale2, shift2, g_ffn, w_out[0].astype(BF16),
        w_router[0].T, b_router.reshape(N_EXPERTS, 1), seq)

    counts = counts[:, 0].astype(I32)
    padded = (counts + MOE_TILE - 1) // MOE_TILE * MOE_TILE
    pend = jnp.cumsum(padded)
    pstart = pend - padded
    idx = info_i[:TOP_K].T
    dest = (pstart[idx] + info_i[TOP_K:].T).reshape(n * TOP_K)
    rows = n * TOP_K + N_EXPERTS * MOE_TILE
    n_tiles = rows // MOE_TILE
    tile_row0 = jnp.arange(n_tiles, dtype=I32) * MOE_TILE
    tile_v = (tile_row0 < pend[-1]).astype(I32)
    last = jnp.maximum(pend[-1] // MOE_TILE - 1, 0)
    tile_x = jnp.minimum(jnp.arange(n_tiles, dtype=I32), last)
    tile_e = jnp.minimum(jnp.sum(pend[None, :] <= (tile_x * MOE_TILE)[:, None], axis=1), N_EXPERTS - 1).astype(I32)
    tile_f = (tile_x * MOE_TILE == pstart[tile_e]).astype(I32) * tile_v
    used = counts[tile_e] - (tile_x * MOE_TILE - pstart[tile_e])
    tile_ns = jnp.clip((used + MOE_SUB - 1) // MOE_SUB, 0, MOE_TILE // MOE_SUB) * tile_v

    xs = _dispatch(pstart + counts, pstart + (counts + MOE_SUB - 1) // MOE_SUB * MOE_SUB, dest, hp, rows)

    src = jnp.arange(256)
    perm = (jnp.arange(256)[None, :] == (src % 2 * LANES + src // 2)[:, None]).astype(BF16)
    f = w1.shape[-1] // 2
    nj = f // MOE_FC
    b1r = b1[0].reshape(N_EXPERTS, nj, 1, MOE_FC, 2)
    items = _moe_items(tile_e, tile_v, tile_f, tile_ns, nj)
    ys = _moe(*items, xs, w1[0], b1r[..., 0], b1r[..., 1], w2[0], b2[0][:, None, :], perm)

    out = _combine(dest, info_w[:TOP_K].T, x1, gate2, g_final[None, :], ys, seq)
    return out.reshape(batch, seq, d)
```

```python
import functools

import jax
import jax.numpy as jnp
from jax import lax
from jax.experimental import pallas as pl
from jax.experimental.pallas import tpu as pltpu

F32 = jnp.float32
BF16 = jnp.bfloat16
I32 = jnp.int32

SB_HEADS = 8
MLA_HEADS = 8
HEAD_DIM = 128
ROPE_DIM = 64
ROPE_HALF = ROPE_DIM // 2
MLA_QK_DIM = HEAD_DIM + ROPE_DIM
LORA = 512
ROPE_THETA = 10000.0
N_EXPERTS = 32
TOP_K = 4
SWIGLU_ALPHA = 1.702
SWIGLU_LIMIT = 7.0
NORM_EPS = 1e-6
SB_WIDTH = SB_HEADS * HEAD_DIM
MLA_WIDTH = MLA_HEADS * HEAD_DIM

LANES = 128
VMEM_LIMIT_BYTES = 56 * 1024 * 1024
MLA_VMEM_LIMIT_BYTES = 60 * 1024 * 1024

ROW_TILE = 256
OUTPROJ_TILE = 512
ATTN_TILE = 256
MLA_Q_TILE = 512
MLA_K_TILE = 512
MLA_BOUND_SLACK = 1.01
MLA_MIN_ROW_SUM = 1e-30
ATTN_HEADS_PER_STEP = 4
MLA_HEADS_PER_STEP = 8
MLA_KV_STRIDE = 3 * HEAD_DIM
SB_DEAD_LOG = -104.0
MOE_TILE = 512
MOE_SUB = 256
MOE_FC = 256
COMBINE_TILE = 256
NEG_BIG = -1e30


def _cparams(sem, vmem_limit_bytes=VMEM_LIMIT_BYTES):
    return pltpu.CompilerParams(dimension_semantics=sem, vmem_limit_bytes=vmem_limit_bytes)


def _resident(shape):
    nd = len(shape)
    return pl.BlockSpec(shape, lambda *_: (0,) * nd, pipeline_mode=pl.Buffered(1))


def _rms(v):
    return v * lax.rsqrt(jnp.mean(v * v, axis=-1, keepdims=True) + NORM_EPS)


def _mod_kernel(c_ref, w_ref, b_ref, o_ref):
    c = c_ref[...]
    a = (c * jax.nn.sigmoid(c)).astype(BF16)
    o_ref[...] = jnp.dot(a, w_ref[...].astype(BF16), preferred_element_type=F32) + b_ref[...]


def _mod(c_pad, w_mod, b_mod):
    rows, d = c_pad.shape
    n_out = w_mod.shape[1]
    tn = 1024
    return pl.pallas_call(
        _mod_kernel,
        out_shape=jax.ShapeDtypeStruct((rows, n_out), F32),
        grid=(n_out // tn,),
        in_specs=[
            pl.BlockSpec((rows, d), lambda j: (0, 0)),
            pl.BlockSpec((d, tn), lambda j: (0, j)),
            pl.BlockSpec((1, tn), lambda j: (0, j)),
        ],
        out_specs=pl.BlockSpec((rows, tn), lambda j: (0, j)),
        compiler_params=_cparams(("arbitrary",)),
        name="mod",
    )(c_pad, w_mod, b_mod)


def _inproj_kernel(x_ref, sc_ref, sh_ref, g_ref, pos_ref, invf_ref, cm_ref, sm2_ref, sm1_ref, w_ref,
                   gq_ref, wq_ref, gkv_ref, wkv_ref,
                   sbq_ref, sbk_ref, sbv_ref, mq_ref, mkv_ref, ksq_ref):
    h = (_rms(x_ref[...]) * g_ref[...] * sc_ref[0] + sh_ref[0]).astype(BF16)
    proj = jnp.dot(h, w_ref[...], preferred_element_type=F32)

    sbq_ref[...] = (proj[:, :SB_WIDTH] * (HEAD_DIM ** -0.5)).astype(BF16)
    sbk_ref[...] = proj[:, SB_WIDTH:2 * SB_WIDTH].astype(BF16)
    sbv_ref[...] = proj[:, 2 * SB_WIDTH:3 * SB_WIDTH].astype(BF16)

    tm = x_ref.shape[0]
    pos = jnp.broadcast_to(pos_ref[...].astype(F32), (LANES, tm))
    ang = pos.T * invf_ref[...]
    cosm = jnp.cos(ang) * cm_ref[...]
    sin = jnp.sin(ang)
    sin_x2 = sin * sm2_ref[...]
    sin_x1 = sin * sm1_ref[...]

    def rope(v):
        return (v * cosm + pltpu.roll(v, LANES - ROPE_HALF, 1) * sin_x2
                + pltpu.roll(v, ROPE_HALF, 1) * sin_x1)

    o = 3 * SB_WIDTH
    q_lat = (_rms(proj[:, o:o + LORA]) * gq_ref[...]).astype(BF16)
    q = jnp.dot(q_lat, wq_ref[...], preferred_element_type=F32) * (MLA_QK_DIM ** -0.5)
    for hd in range(MLA_HEADS):
        c0 = hd * 2 * HEAD_DIM
        mq_ref[:, c0:c0 + HEAD_DIM] = q[:, c0:c0 + HEAD_DIM].astype(BF16)
        mq_ref[:, c0 + HEAD_DIM:c0 + 2 * HEAD_DIM] = rope(q[:, c0 + HEAD_DIM:c0 + 2 * HEAD_DIM]).astype(BF16)

    kv_lat = (_rms(proj[:, o + LORA:o + 2 * LORA]) * gkv_ref[...]).astype(BF16)
    kv = jnp.dot(kv_lat, wkv_ref[...], preferred_element_type=F32).astype(BF16)
    kr = proj[:, o + 2 * LORA:o + 2 * LORA + ROPE_DIM]
    k_rope = rope(jnp.concatenate([kr, jnp.zeros((tm, LANES - ROPE_DIM), F32)], axis=1)).astype(BF16)
    kr32 = k_rope.astype(F32)
    rope_sq = jnp.sum(kr32 * kr32, axis=1, keepdims=True)
    key_sq = []
    for hd in range(MLA_HEADS):
        c0 = hd * MLA_KV_STRIDE
        k_nope = kv[:, hd * 2 * HEAD_DIM:hd * 2 * HEAD_DIM + HEAD_DIM]
        mkv_ref[:, c0:c0 + HEAD_DIM] = k_nope
        mkv_ref[:, c0 + HEAD_DIM:c0 + 2 * HEAD_DIM] = k_rope
        mkv_ref[:, c0 + 2 * HEAD_DIM:c0 + 3 * HEAD_DIM] = kv[:, hd * 2 * HEAD_DIM + HEAD_DIM:(hd + 1) * 2 * HEAD_DIM]
        kn32 = k_nope.astype(F32)
        sq = jnp.sum(kn32 * kn32, axis=1, keepdims=True) + rope_sq
        key_sq.append(jnp.broadcast_to(jnp.max(sq, axis=0, keepdims=True), (1, LANES)))
    ksq_ref[0] = jnp.concatenate(key_sq, axis=0)


def _inproj(x2, sc1, sh1, g_attn, pos, rope_tables, w_all, g_q, wq, g_kv, wkv, seq):
    n, d = x2.shape
    tm = ROW_TILE
    wcols = w_all.shape[1]
    per_b = seq // tm
    row = lambda i: (i, 0)
    bat = lambda i: (i // per_b, 0, 0)
    outs = [
        jax.ShapeDtypeStruct((n, SB_WIDTH), BF16),
        jax.ShapeDtypeStruct((n, SB_WIDTH), BF16),
        jax.ShapeDtypeStruct((n, SB_WIDTH), BF16),
        jax.ShapeDtypeStruct((n, 2 * MLA_WIDTH), BF16),
        jax.ShapeDtypeStruct((n, MLA_HEADS * MLA_KV_STRIDE), BF16),
        jax.ShapeDtypeStruct((n // tm, MLA_HEADS, LANES), F32),
    ]
    return pl.pallas_call(
        _inproj_kernel,
        out_shape=outs,
        grid=(n // tm,),
        in_specs=[
            pl.BlockSpec((tm, d), row),
            pl.BlockSpec((1, 1, d), bat),
            pl.BlockSpec((1, 1, d), bat),
            _resident((1, d)),
            pl.BlockSpec((1, tm), lambda i: (0, i)),
            _resident((1, LANES)),
            _resident((1, LANES)),
            _resident((1, LANES)),
            _resident((1, LANES)),
            _resident((d, wcols)),
            _resident((1, LORA)),
            _resident((LORA, 2 * MLA_WIDTH)),
            _resident((1, LORA)),
            _resident((LORA, 2 * MLA_WIDTH)),
        ],
        out_specs=[
            pl.BlockSpec((tm, SB_WIDTH), row),
            pl.BlockSpec((tm, SB_WIDTH), row),
            pl.BlockSpec((tm, SB_WIDTH), row),
            pl.BlockSpec((tm, 2 * MLA_WIDTH), row),
            pl.BlockSpec((tm, MLA_HEADS * MLA_KV_STRIDE), row),
            pl.BlockSpec((1, MLA_HEADS, LANES), lambda i: (i, 0, 0)),
        ],
        compiler_params=_cparams(("arbitrary",)),
        name="inproj",
    )(x2, sc1, sh1, g_attn, pos, *rope_tables, w_all, g_q, wq, g_kv, wkv)


_NT = (((1,), (1,)), ((), ()))


def _sb_kernel(q_ref, k_ref, v_ref, o_ref, *, tile, heads):
    i = pl.program_id(2)
    row = lax.broadcasted_iota(I32, (tile, tile), 0)
    col = lax.broadcasted_iota(I32, (tile, tile), 1)
    suffix = jnp.where(row >= col, 1.0, 0.0).astype(BF16)
    strict = col < row

    def head_step(h, kt, c, acc, diag):
        hs = slice(h * HEAD_DIM, (h + 1) * HEAD_DIM)
        k0 = pl.multiple_of(kt * tile, tile)
        k = k_ref[pl.ds(k0, tile), hs]
        v = v_ref[pl.ds(k0, tile), hs]
        z = lax.dot_general(q_ref[:, hs], k, _NT, preferred_element_type=F32)
        log_beta = jnp.minimum(z, 0.0) - jnp.log(1.0 + jnp.exp(-jnp.abs(z)))
        log_keep = log_beta - z
        if diag:
            log_keep = jnp.where(strict, log_keep, 0.0)
        incl = jnp.dot(log_keep.astype(BF16), suffix, preferred_element_type=F32)
        a = jnp.exp(incl + z + c)
        if diag:
            a = jnp.where(strict, a, 0.0)
        acc = acc + jnp.dot(a.astype(BF16), v, preferred_element_type=F32)
        return c + incl[:, 0:1], acc

    def step(kt, carry, diag):
        return tuple(head_step(h, kt, *carry[h], diag) for h in range(heads))

    def live(carry):
        c_max = carry[0][0]
        for h in range(1, heads):
            c_max = jnp.maximum(c_max, carry[h][0])
        return (jnp.max(c_max) > SB_DEAD_LOG).astype(I32)

    zero = (jnp.zeros((tile, 1), F32), jnp.zeros((tile, HEAD_DIM), F32))
    carry = step(i, (zero,) * heads, True)

    def body(state):
        kt, _, cr = state
        cr = step(kt, cr, False)
        return kt - 1, live(cr), cr

    _, _, carry = lax.while_loop(lambda st: (st[0] >= 0) & (st[1] > 0), body, (i - 1, live(carry), carry))
    for h in range(heads):
        o_ref[:, h * HEAD_DIM:(h + 1) * HEAD_DIM] = carry[h][1].astype(BF16)


def _sb_attention(sbq, sbk, sbv, batch, seq):
    n = sbq.shape[0]
    t = ATTN_TILE
    hb = ATTN_HEADS_PER_STEP
    nq = seq // t
    qmap = lambda b, h, i: (b * nq + i, h)
    kmap = lambda b, h, i: (b, h)
    return pl.pallas_call(
        functools.partial(_sb_kernel, tile=t, heads=hb),
        out_shape=jax.ShapeDtypeStruct((n, SB_WIDTH), BF16),
        grid=(batch, SB_HEADS // hb, nq),
        in_specs=[
            pl.BlockSpec((t, hb * HEAD_DIM), qmap),
            pl.BlockSpec((seq, hb * HEAD_DIM), kmap),
            pl.BlockSpec((seq, hb * HEAD_DIM), kmap),
        ],
        out_specs=pl.BlockSpec((t, hb * HEAD_DIM), qmap),
        compiler_params=_cparams(("arbitrary", "arbitrary", "arbitrary")),
        name="sb_attn",
    )(sbq, sbk, sbv)


def _mla_kernel(q_ref, kv_ref, kn_ref, o_ref, *, tq, tk, heads):
    i = pl.program_id(2)
    row = lax.broadcasted_iota(I32, (tq, tk), 0)
    col = lax.broadcasted_iota(I32, (tq, tk), 1)
    qk = 2 * HEAD_DIM
    per_q = tq // tk

    def scores(h, kt, diag):
        k0 = pl.multiple_of(kt * tk, tk)
        k = kv_ref[pl.ds(k0, tk), h * MLA_KV_STRIDE:h * MLA_KV_STRIDE + qk]
        s = lax.dot_general(q_ref[:, h * qk:(h + 1) * qk], k, _NT, preferred_element_type=F32)
        if diag:
            s = jnp.where(col + kt * tk <= row + i * tq, s, NEG_BIG)
        return s

    def lane_fold(x, op):
        out = x[:, :LANES]
        for cblk in range(1, tk // LANES):
            out = op(out, x[:, cblk * LANES:(cblk + 1) * LANES])
        return out

    def over_keys(step, carry):
        for r in range(per_q):
            carry = step(i * per_q + r, carry, True)
        return lax.fori_loop(0, i * per_q, lambda s, cr: step(s, cr, False), carry)

    def softmax_pv(shift):
        def acc_step(kt, carry, diag):
            out = []
            for h in range(heads):
                l, acc = carry[h]
                k0 = pl.multiple_of(kt * tk, tk)
                v = kv_ref[pl.ds(k0, tk), h * MLA_KV_STRIDE + qk:(h + 1) * MLA_KV_STRIDE]
                s = scores(h, kt, diag)
                p = jnp.concatenate([jnp.exp(s[:, c * LANES:(c + 1) * LANES] - shift[h])
                                     for c in range(tk // LANES)], axis=1)
                out.append((l + lane_fold(p, jnp.add),
                            acc + jnp.dot(p.astype(BF16), v, preferred_element_type=F32)))
            return tuple(out)

        return over_keys(acc_step, ((jnp.zeros((tq, LANES), F32), jnp.zeros((tq, HEAD_DIM), F32)),) * heads)

    def row_sums(carry):
        return tuple(jnp.sum(carry[h][0], axis=1, keepdims=True) for h in range(heads))

    def write(carry, sums):
        for h in range(heads):
            o_ref[:, h * HEAD_DIM:(h + 1) * HEAD_DIM] = (carry[h][1] / sums[h]).astype(BF16)

    head0 = pl.program_id(1) * heads
    bound = []
    for h in range(heads):
        qh = q_ref[:, h * qk:(h + 1) * qk].astype(F32)
        q_norm = jnp.sqrt(jnp.sum(qh * qh, axis=1, keepdims=True))
        bound.append(q_norm * kn_ref[0, pl.ds(head0 + h, 1), :] * MLA_BOUND_SLACK)
    fast = softmax_pv(tuple(bound))
    fast_sums = row_sums(fast)
    smallest = fast_sums[0]
    for h in range(1, heads):
        smallest = jnp.minimum(smallest, fast_sums[h])
    in_range = jnp.min(smallest) > MLA_MIN_ROW_SUM
    write(fast, fast_sums)

    @pl.when(jnp.logical_not(in_range))
    def _():
        def max_step(kt, ms, diag):
            return tuple(jnp.maximum(ms[h], lane_fold(scores(h, kt, diag), jnp.maximum)) for h in range(heads))

        ms = over_keys(max_step, (jnp.full((tq, LANES), NEG_BIG, F32),) * heads)
        row_max = tuple(jnp.broadcast_to(jnp.max(ms[h], axis=1, keepdims=True), (tq, LANES))
                        for h in range(heads))
        exact = softmax_pv(row_max)
        write(exact, row_sums(exact))


def _mla_attention(mq, mkv, k_norm, batch, seq):
    n = mq.shape[0]
    t = MLA_Q_TILE
    hb = MLA_HEADS_PER_STEP
    nq = seq // t
    return pl.pallas_call(
        functools.partial(_mla_kernel, tq=t, tk=MLA_K_TILE, heads=hb),
        out_shape=jax.ShapeDtypeStruct((n, MLA_WIDTH), BF16),
        grid=(batch, MLA_HEADS // hb, nq),
        in_specs=[
            pl.BlockSpec((t, hb * 2 * HEAD_DIM), lambda b, h, i: (b * nq + i, h)),
            pl.BlockSpec((seq, hb * MLA_KV_STRIDE), lambda b, h, i: (b, h), pipeline_mode=pl.Buffered(1)),
            pl.BlockSpec((1, MLA_HEADS, LANES), lambda b, h, i: (b, 0, 0)),
        ],
        out_specs=pl.BlockSpec((t, hb * HEAD_DIM), lambda b, h, i: (b * nq + i, h)),
        compiler_params=_cparams(("arbitrary", "arbitrary", "arbitrary"), MLA_VMEM_LIMIT_BYTES),
        name="mla_attn",
    )(mq, mkv, k_norm)


def _outproj_kernel(sb_ref, mla_ref, x_ref, g1_ref, sc_ref, sh_ref, g_ref, wo_ref, wr_ref, br_ref,
                    x1_ref, hp_ref, ii_ref, iw_ref, cnt_ref, run_ref):
    step = pl.program_id(0)
    tm, d = x_ref.shape

    @pl.when(step == 0)
    def _():
        run_ref[...] = jnp.zeros_like(run_ref)

    o = (jnp.dot(sb_ref[...], wo_ref[:SB_WIDTH, :], preferred_element_type=F32)
         + jnp.dot(mla_ref[...], wo_ref[SB_WIDTH:, :], preferred_element_type=F32))
    x1 = x_ref[...] + g1_ref[0] * o
    x1_ref[...] = x1
    h2 = _rms(x1) * g_ref[...] * sc_ref[0] + sh_ref[0]
    hb = h2.astype(BF16)
    hb32 = hb.astype(F32)
    hp_ref[...] = hb32

    h_lo = (h2 - hb32).astype(BF16)
    w = wr_ref[...]
    w_hi = w.astype(BF16)
    w_lo = (w - w_hi.astype(F32)).astype(BF16)
    logits = (lax.dot_general(w_hi, hb, _NT, preferred_element_type=F32)
              + lax.dot_general(w_hi, h_lo, _NT, preferred_element_type=F32)
              + lax.dot_general(w_lo, hb, _NT, preferred_element_type=F32)) + br_ref[...]

    expert = lax.broadcasted_iota(I32, (N_EXPERTS, tm), 0)
    work = logits
    vals, idxs, hots = [], [], []
    for _ in range(TOP_K):
        m = jnp.max(work, axis=0, keepdims=True)
        idx = jnp.min(jnp.where(work == m, expert, N_EXPERTS), axis=0, keepdims=True)
        hot = expert == idx
        work = jnp.where(hot, -jnp.inf, work)
        vals.append(m)
        idxs.append(idx)
        hots.append(hot)
    exps = [jnp.exp(v - vals[0]) for v in vals]
    den = exps[0] + exps[1] + exps[2] + exps[3]
    sel = jnp.where(hots[0] | hots[1] | hots[2] | hots[3], 1.0, 0.0)

    r = lax.broadcasted_iota(I32, (tm, tm), 0)
    c = lax.broadcasted_iota(I32, (tm, tm), 1)
    earlier = jnp.where(r < c, 1.0, 0.0).astype(BF16)
    rank = jnp.dot(sel.astype(BF16), earlier, preferred_element_type=F32) + run_ref[...]
    run_ref[...] = run_ref[...] + jnp.sum(sel, axis=1, keepdims=True)
    cnt_ref[...] = run_ref[...]

    field = lax.broadcasted_iota(I32, (2 * TOP_K, tm), 0)
    info_i = jnp.zeros((2 * TOP_K, tm), I32)
    info_w = jnp.zeros((2 * TOP_K, tm), F32)
    for k in range(TOP_K):
        rank_k = jnp.sum(jnp.where(hots[k], rank, 0.0), axis=0, keepdims=True).astype(I32)
        info_i = jnp.where(field == k, idxs[k], info_i)
        info_i = jnp.where(field == TOP_K + k, rank_k, info_i)
        info_w = jnp.where(field == k, exps[k] / den, info_w)
    ii_ref[...] = info_i
    iw_ref[...] = info_w


def _outproj(sb, mla, x2, g1, sc2, sh2, g_ffn, wo, w_router, b_router, seq):
    n, d = x2.shape
    tm = OUTPROJ_TILE
    per_b = seq // tm
    row = lambda i: (i, 0)
    bat = lambda i: (i // per_b, 0, 0)
    outs = [
        jax.ShapeDtypeStruct((n, d), F32),
        jax.ShapeDtypeStruct((n, d), F32),
        jax.ShapeDtypeStruct((2 * TOP_K, n), I32),
        jax.ShapeDtypeStruct((2 * TOP_K, n), F32),
        jax.ShapeDtypeStruct((N_EXPERTS, 1), F32),
    ]
    return pl.pallas_call(
        _outproj_kernel,
        out_shape=outs,
        grid=(n // tm,),
        in_specs=[
            pl.BlockSpec((tm, SB_WIDTH), row),
            pl.BlockSpec((tm, MLA_WIDTH), row),
            pl.BlockSpec((tm, d), row),
            pl.BlockSpec((1, 1, d), bat),
            pl.BlockSpec((1, 1, d), bat),
            pl.BlockSpec((1, 1, d), bat),
            _resident((1, d)),
            _resident((SB_WIDTH + MLA_WIDTH, d)),
            _resident((N_EXPERTS, d)),
            _resident((N_EXPERTS, 1)),
        ],
        out_specs=[
            pl.BlockSpec((tm, d), row),
            pl.BlockSpec((tm, d), row),
            pl.BlockSpec((2 * TOP_K, tm), lambda i: (0, i)),
            pl.BlockSpec((2 * TOP_K, tm), lambda i: (0, i)),
            pl.BlockSpec((N_EXPERTS, 1), lambda i: (0, 0)),
        ],
        scratch_shapes=[pltpu.VMEM((N_EXPERTS, 1), F32)],
        compiler_params=_cparams(("arbitrary",)),
        name="outproj",
    )(sb, mla, x2, g1, sc2, sh2, g_ffn, wo, w_router, b_router)


def _row_copy(src_ref, src_row, dst_ref, dst_row, sem):
    return pltpu.make_async_copy(src_ref.at[pl.ds(src_row, 1), :], dst_ref.at[pl.ds(dst_row, 1), :], sem)


def _dispatch_kernel(pad_lo_ref, pad_hi_ref, dest_ref, h_ref, xs_ref, zero_ref, sem):
    tm = h_ref.shape[0]

    @pl.when(pl.program_id(0) == 0)
    def _():
        zero_ref[...] = jnp.zeros_like(zero_ref)

        def put(r, c):
            _row_copy(zero_ref, 0, xs_ref, r, sem).start()
            return c

        def got(r, c):
            _row_copy(zero_ref, 0, xs_ref, 0, sem).wait()
            return c

        def over_pad_rows(fn):
            def per_expert(e, carry):
                return lax.fori_loop(pad_lo_ref[e], pad_hi_ref[e], fn, carry)

            lax.fori_loop(0, N_EXPERTS, per_expert, 0)

        over_pad_rows(put)
        over_pad_rows(got)

    def issue(r, carry):
        for k in range(TOP_K):
            _row_copy(h_ref, r, xs_ref, dest_ref[r * TOP_K + k], sem).start(priority=k % 2)
        return carry

    lax.fori_loop(0, tm, issue, 0)
    for k in range(TOP_K):
        pltpu.make_async_copy(h_ref, xs_ref.at[pl.ds(0, tm), :], sem).wait()


def _dispatch(pad_lo, pad_hi, dest_flat, hp, rows):
    n, w = hp.shape
    tm = ROW_TILE
    grid_spec = pltpu.PrefetchScalarGridSpec(
        num_scalar_prefetch=2,
        grid=(n // tm,),
        in_specs=[
            pl.BlockSpec((tm * TOP_K,), lambda i, lo, hi: (i,), memory_space=pltpu.SMEM),
            pl.BlockSpec((tm, w), lambda i, lo, hi: (i, 0)),
        ],
        out_specs=pl.BlockSpec(memory_space=pl.ANY),
        scratch_shapes=[pltpu.VMEM((8, w), F32), pltpu.SemaphoreType.DMA(())],
    )
    return pl.pallas_call(
        _dispatch_kernel,
        out_shape=jax.ShapeDtypeStruct((rows, w), F32),
        grid_spec=grid_spec,
        compiler_params=_cparams(("arbitrary",)),
        name="dispatch",
    )(pad_lo, pad_hi, dest_flat, hp)


def _moe_kernel(ie_ref, ix_ref, ic_ref, iv_ref, ns_ref, xs_ref, w1_ref, b1g_ref, b1l_ref, w2_ref, b2_ref,
                p_ref, ys_ref, xb_ref, gl_ref, w2b_ref):
    del ie_ref, ix_ref
    item = pl.program_id(0)
    nj, fc, _ = w2b_ref.shape
    chunk = ic_ref[item]

    def ffn(j, rows):
        xt = xb_ref[:rows, :]
        ug = jnp.dot(xt, gl_ref[j, :, :fc], preferred_element_type=F32) + b1g_ref[j]
        ul = jnp.dot(xt, gl_ref[j, :, fc:], preferred_element_type=F32) + b1l_ref[j]
        glu = jnp.minimum(ug, SWIGLU_LIMIT)
        lin = jnp.clip(ul, -SWIGLU_LIMIT, SWIGLU_LIMIT)
        act = glu * jax.nn.sigmoid(SWIGLU_ALPHA * glu) * (lin + 1.0)
        ys_ref[:rows, :] += jnp.dot(act.astype(BF16), w2b_ref[j], preferred_element_type=F32)

    def for_used_rows(fn):
        for n_sub in range(1, MOE_TILE // MOE_SUB + 1):
            pl.when(ns_ref[item] == n_sub)(functools.partial(fn, n_sub * MOE_SUB))

    @pl.when(iv_ref[item] == 1)
    def _():
        @pl.when((chunk == 0) | (chunk == nj))
        def _():
            def load_rows(rows):
                xb_ref[:rows, :] = xs_ref[:rows, :].astype(BF16)

            for_used_rows(load_rows)
            ys_ref[...] = jnp.broadcast_to(b2_ref[...], ys_ref.shape)

        @pl.when(chunk < nj)
        def _():
            p = p_ref[...]
            for blk in range(2 * fc // 256):
                wb = w1_ref[:, blk * 256:(blk + 1) * 256].astype(BF16)
                split = jnp.dot(wb, p, preferred_element_type=F32).astype(BF16)
                gl_ref[chunk, :, blk * LANES:(blk + 1) * LANES] = split[:, :LANES]
                gl_ref[chunk, :, fc + blk * LANES:fc + (blk + 1) * LANES] = split[:, LANES:]
            w2b_ref[chunk] = w2_ref[...].astype(BF16)
            for_used_rows(functools.partial(ffn, chunk))

        @pl.when(chunk == nj)
        def _():
            def all_chunks(rows):
                def body(j, carry):
                    ffn(j, rows)
                    return carry

                lax.fori_loop(0, nj, body, 0, unroll=4)

            for_used_rows(all_chunks)


def _lookup(table, idx):
    hit = idx[..., None] == jnp.arange(table.shape[0], dtype=idx.dtype)
    return jnp.sum(jnp.where(hit, table, 0), axis=-1)


def _moe_items(tile_e, tile_v, tile_f, tile_ns, nj):
    n_tiles = tile_v.shape[0]
    n_items = n_tiles + N_EXPERTS * (nj - 1)
    cost = jnp.where(tile_f == 1, nj, 1) * tile_v
    first_item = jnp.cumsum(cost) - cost
    total = jnp.sum(cost)
    last_tile = jnp.maximum(jnp.sum(tile_v) - 1, 0)
    item = jnp.arange(n_items, dtype=I32)
    valid = item < total
    tile = jnp.where(valid, jnp.sum(first_item[None, :] <= item[:, None], axis=1) - 1, last_tile).astype(I32)
    chunk = jnp.where(valid & (_lookup(tile_f, tile) == 1), item - _lookup(first_item, tile), nj).astype(I32)
    return _lookup(tile_e, tile), tile, chunk, valid.astype(I32), (_lookup(tile_ns, tile) * valid).astype(I32)


def _moe(item_e, item_x, item_c, item_v, item_ns, xs, w1, b1g, b1l, w2, b2, perm):
    rows, d = xs.shape
    f = w1.shape[2] // 2
    fc = MOE_FC
    nj = f // fc

    def wj(q, ic):
        return jnp.minimum(ic[q], nj - 1)

    grid_spec = pltpu.PrefetchScalarGridSpec(
        num_scalar_prefetch=5,
        grid=(item_e.shape[0],),
        in_specs=[
            pl.BlockSpec((MOE_TILE, d), lambda q, ie, ix, ic, iv, ns: (ix[q], 0)),
            pl.BlockSpec((None, d, 2 * fc), lambda q, ie, ix, ic, iv, ns: (ie[q], 0, wj(q, ic))),
            pl.BlockSpec((None, nj, 1, fc), lambda q, ie, ix, ic, iv, ns: (ie[q], 0, 0, 0)),
            pl.BlockSpec((None, nj, 1, fc), lambda q, ie, ix, ic, iv, ns: (ie[q], 0, 0, 0)),
            pl.BlockSpec((None, fc, d), lambda q, ie, ix, ic, iv, ns: (ie[q], wj(q, ic), 0)),
            pl.BlockSpec((None, 1, d), lambda q, ie, ix, ic, iv, ns: (ie[q], 0, 0)),
            pl.BlockSpec((256, 256), lambda q, ie, ix, ic, iv, ns: (0, 0)),
        ],
        out_specs=pl.BlockSpec((MOE_TILE, d), lambda q, ie, ix, ic, iv, ns: (ix[q], 0)),
        scratch_shapes=[
            pltpu.VMEM((MOE_TILE, d), BF16),
            pltpu.VMEM((nj, d, 2 * fc), BF16),
            pltpu.VMEM((nj, fc, d), BF16),
        ],
    )
    return pl.pallas_call(
        _moe_kernel,
        out_shape=jax.ShapeDtypeStruct((rows, d), F32),
        grid_spec=grid_spec,
        compiler_params=_cparams(("arbitrary",)),
        name="moe",
    )(item_e, item_x, item_c, item_v, item_ns, xs, w1, b1g, b1l, w2, b2, perm)


def _combine_kernel(dest_ref, next_ref, w_ref, x1_ref, g2_ref, gf_ref, ys_ref, o_ref, buf_ref, sem):
    i = pl.program_id(0)
    tm = x1_ref.shape[0]

    def gather(idx_ref, s):
        def issue(r, carry):
            for k in range(TOP_K):
                _row_copy(ys_ref, idx_ref[r * TOP_K + k], buf_ref.at[s], k * tm + r, sem.at[s]).start(priority=k % 2)
            return carry

        lax.fori_loop(0, tm, issue, 0)

    @pl.when(i == 0)
    def _():
        gather(dest_ref, 0)

    def reduce_slot(s):
        @pl.when(i + 1 < pl.num_programs(0))
        def _():
            gather(next_ref, 1 - s)

        pltpu.make_async_copy(ys_ref.at[pl.ds(0, TOP_K * tm), :], buf_ref.at[s], sem.at[s]).wait()
        w = w_ref[...]
        moe = w[:, 0:1] * buf_ref[s, :tm, :]
        for k in range(1, TOP_K):
            moe = moe + w[:, k:k + 1] * buf_ref[s, k * tm:(k + 1) * tm, :]
        x = x1_ref[...] + g2_ref[0] * moe
        o_ref[...] = _rms(x) * gf_ref[...]

    for s in range(2):
        pl.when(i % 2 == s)(functools.partial(reduce_slot, s))


def _combine(dest_flat, weights, x1, g2, g_final, ys, seq):
    n, d = x1.shape
    tm = COMBINE_TILE
    per_b = seq // tm
    steps = n // tm
    return pl.pallas_call(
        _combine_kernel,
        out_shape=jax.ShapeDtypeStruct((n, d), F32),
        grid=(steps,),
        in_specs=[
            pl.BlockSpec((tm * TOP_K,), lambda i: (i,), memory_space=pltpu.SMEM),
            pl.BlockSpec((tm * TOP_K,), lambda i: (jnp.minimum(i + 1, steps - 1),), memory_space=pltpu.SMEM),
            pl.BlockSpec((tm, TOP_K), lambda i: (i, 0)),
            pl.BlockSpec((tm, d), lambda i: (i, 0)),
            pl.BlockSpec((1, 1, d), lambda i: (i // per_b, 0, 0)),
            _resident((1, d)),
            pl.BlockSpec(memory_space=pl.ANY),
        ],
        out_specs=pl.BlockSpec((tm, d), lambda i: (i, 0)),
        scratch_shapes=[pltpu.VMEM((2, TOP_K * tm, d), F32), pltpu.SemaphoreType.DMA((2,))],
        compiler_params=_cparams(("arbitrary",)),
        name="combine",
    )(dest_flat, dest_flat, weights, x1, g2, g_final, ys)


def _rope_lane_tables():
    inv_freq = 1.0 / (ROPE_THETA ** (jnp.arange(ROPE_HALF, dtype=F32) * 2.0 / ROPE_DIM))
    z = jnp.zeros((ROPE_HALF,), F32)
    o = jnp.ones((ROPE_HALF,), F32)
    invf = jnp.concatenate([inv_freq, inv_freq, z, z])[None, :]
    cmask = jnp.concatenate([o, o, z, z])[None, :]
    smask_x2 = jnp.concatenate([-o, z, z, z])[None, :]
    smask_x1 = jnp.concatenate([z, o, z, z])[None, :]
    return invf, cmask, smask_x2, smask_x1


def _pad_rope_cols(w):
    return jnp.concatenate([w, jnp.zeros(w.shape[:-1] + (LANES - ROPE_DIM,), w.dtype)], axis=-1)


def kernel(x, c, positions, g_attn, w_mod, b_mod, w_in, g_q_lat, w_q_up, g_kv_lat, w_kv_up, w_out, g_ffn,
           w_router, b_router, w1, b1, w2, b2, g_final):
    batch, seq, d = x.shape
    n = batch * seq
    depth = w_mod.shape[0]
    assert depth == 1 and seq % ATTN_TILE == 0 and seq % ROW_TILE == 0 and d == SB_WIDTH + MLA_WIDTH
    x2 = x.reshape(n, d)

    c_pad = jnp.zeros((8, d), F32).at[:batch].set(c)
    mod = _mod(c_pad, w_mod[0], b_mod[0][None, :])[:batch]
    shift1, scale1, gate1, shift2, scale2, gate2 = [m[:, None, :] for m in jnp.split(mod, 6, axis=-1)]

    w_all = w_in[0].astype(BF16)
    wq = w_q_up[0].reshape(LORA, MLA_HEADS, MLA_QK_DIM)
    wq = jnp.concatenate([wq[..., :HEAD_DIM], _pad_rope_cols(wq[..., HEAD_DIM:])], axis=-1)
    wq = wq.reshape(LORA, MLA_HEADS * 2 * HEAD_DIM).astype(BF16)
    wkv = w_kv_up[0].astype(BF16)

    sbq, sbk, sbv, mq, mkv, key_sq = _inproj(
        x2, 1.0 + scale1, shift1, g_attn, positions.reshape(1, n), _rope_lane_tables(), w_all,
        g_q_lat, wq, g_kv_lat, wkv, seq)

    sb_out = _sb_attention(sbq, sbk, sbv, batch, seq)
    k_norm = jnp.sqrt(jnp.max(key_sq.reshape(batch, -1, MLA_HEADS, LANES), axis=1))
    mla_out = _mla_attention(mq, mkv, k_norm, batch, seq)

    x1, hp, info_i, info_w, counts = _outproj(
        sb_out, mla_out, x2, gate1, 1.0 + scale2, shift2, g_ffn, w_out[0].astype(BF16),
        w_router[0].T, b_router.reshape(N_EXPERTS, 1), seq)

    counts = counts[:, 0].astype(I32)
    padded = (counts + MOE_TILE - 1) // MOE_TILE * MOE_TILE
    pend = jnp.cumsum(padded)
    pstart = pend - padded
    idx = info_i[:TOP_K].T
    dest = (_lookup(pstart, idx) + info_i[TOP_K:].T).reshape(n * TOP_K)
    rows = n * TOP_K + N_EXPERTS * MOE_TILE
    n_tiles = rows // MOE_TILE
    tile_row0 = jnp.arange(n_tiles, dtype=I32) * MOE_TILE
    tile_v = (tile_row0 < pend[-1]).astype(I32)
    last = jnp.maximum(pend[-1] // MOE_TILE - 1, 0)
    tile_x = jnp.minimum(jnp.arange(n_tiles, dtype=I32), last)
    tile_e = jnp.minimum(jnp.sum(pend[None, :] <= (tile_x * MOE_TILE)[:, None], axis=1), N_EXPERTS - 1).astype(I32)
    tile_start = _lookup(pstart, tile_e)
    tile_f = (tile_x * MOE_TILE == tile_start).astype(I32) * tile_v
    used = _lookup(counts, tile_e) - (tile_x * MOE_TILE - tile_start)
    tile_ns = jnp.clip((used + MOE_SUB - 1) // MOE_SUB, 0, MOE_TILE // MOE_SUB) * tile_v

    xs = _dispatch(pstart + counts, pstart + (counts + MOE_SUB - 1) // MOE_SUB * MOE_SUB, dest, hp, rows)

    src = jnp.arange(256)
    perm = (jnp.arange(256)[None, :] == (src % 2 * LANES + src // 2)[:, None]).astype(BF16)
    f = w1.shape[-1] // 2
    nj = f // MOE_FC
    b1r = b1[0].reshape(N_EXPERTS, nj, 1, MOE_FC, 2)
    items = _moe_items(tile_e, tile_v, tile_f, tile_ns, nj)
    ys = _moe(*items, xs, w1[0], b1r[..., 0], b1r[..., 1], w2[0], b2[0][:, None, :], perm)

    out = _combine(dest, info_w[:TOP_K].T, x1, gate2, g_final[None, :], ys, seq)
    return out.reshape(batch, seq, d)
```

```python
import functools

import jax
import jax.numpy as jnp
from jax import lax
from jax.experimental import pallas as pl
from jax.experimental.pallas import tpu as pltpu

F32 = jnp.float32
BF16 = jnp.bfloat16
I32 = jnp.int32

SB_HEADS = 8
MLA_HEADS = 8
HEAD_DIM = 128
ROPE_DIM = 64
ROPE_HALF = ROPE_DIM // 2
MLA_QK_DIM = HEAD_DIM + ROPE_DIM
LORA = 512
ROPE_THETA = 10000.0
N_EXPERTS = 32
TOP_K = 4
SWIGLU_ALPHA = 1.702
SWIGLU_LIMIT = 7.0
NORM_EPS = 1e-6
SB_WIDTH = SB_HEADS * HEAD_DIM
MLA_WIDTH = MLA_HEADS * HEAD_DIM

LANES = 128
VMEM_LIMIT_BYTES = 56 * 1024 * 1024
MLA_VMEM_LIMIT_BYTES = 60 * 1024 * 1024

ROW_TILE = 256
OUTPROJ_TILE = 512
ATTN_TILE = 256
MLA_Q_TILE = 512
MLA_K_TILE = 512
MLA_BOUND_SLACK = 1.01
MLA_MIN_ROW_SUM = 1e-30
ATTN_HEADS_PER_STEP = 4
MLA_HEADS_PER_STEP = 8
MLA_KV_STRIDE = 3 * HEAD_DIM
SB_DEAD_LOG = -104.0
MOE_TILE = 512
MOE_SUB = 256
MOE_FC = 256
COMBINE_TILE = 256
NEG_BIG = -1e30


def _cparams(sem, vmem_limit_bytes=VMEM_LIMIT_BYTES):
    return pltpu.CompilerParams(dimension_semantics=sem, vmem_limit_bytes=vmem_limit_bytes)


def _resident(shape):
    nd = len(shape)
    return pl.BlockSpec(shape, lambda *_: (0,) * nd, pipeline_mode=pl.Buffered(1))


def _rms(v):
    return v * lax.rsqrt(jnp.mean(v * v, axis=-1, keepdims=True) + NORM_EPS)


def _mod_kernel(c_ref, w_ref, b_ref, o_ref):
    c = c_ref[...]
    a = (c * jax.nn.sigmoid(c)).astype(BF16)
    o_ref[...] = jnp.dot(a, w_ref[...].astype(BF16), preferred_element_type=F32) + b_ref[...]


def _mod(c_pad, w_mod, b_mod):
    rows, d = c_pad.shape
    n_out = w_mod.shape[1]
    tn = 1024
    return pl.pallas_call(
        _mod_kernel,
        out_shape=jax.ShapeDtypeStruct((rows, n_out), F32),
        grid=(n_out // tn,),
        in_specs=[
            pl.BlockSpec((rows, d), lambda j: (0, 0)),
            pl.BlockSpec((d, tn), lambda j: (0, j)),
            pl.BlockSpec((1, tn), lambda j: (0, j)),
        ],
        out_specs=pl.BlockSpec((rows, tn), lambda j: (0, j)),
        compiler_params=_cparams(("arbitrary",)),
        name="mod",
    )(c_pad, w_mod, b_mod)


def _inproj_kernel(x_ref, sc_ref, sh_ref, g_ref, pos_ref, invf_ref, cm_ref, sm2_ref, sm1_ref, w_ref,
                   gq_ref, wq_ref, gkv_ref, wkv_ref,
                   sbq_ref, sbk_ref, sbv_ref, mq_ref, mkv_ref, ksq_ref):
    h = (_rms(x_ref[...]) * g_ref[...] * sc_ref[0] + sh_ref[0]).astype(BF16)
    proj = jnp.dot(h, w_ref[...], preferred_element_type=F32)

    sbq_ref[...] = (proj[:, :SB_WIDTH] * (HEAD_DIM ** -0.5)).astype(BF16)
    sbk_ref[...] = proj[:, SB_WIDTH:2 * SB_WIDTH].astype(BF16)
    sbv_ref[...] = proj[:, 2 * SB_WIDTH:3 * SB_WIDTH].astype(BF16)

    tm = x_ref.shape[0]
    pos = jnp.broadcast_to(pos_ref[...].astype(F32), (LANES, tm))
    ang = pos.T * invf_ref[...]
    cosm = jnp.cos(ang) * cm_ref[...]
    sin = jnp.sin(ang)
    sin_x2 = sin * sm2_ref[...]
    sin_x1 = sin * sm1_ref[...]

    def rope(v):
        return (v * cosm + pltpu.roll(v, LANES - ROPE_HALF, 1) * sin_x2
                + pltpu.roll(v, ROPE_HALF, 1) * sin_x1)

    o = 3 * SB_WIDTH
    q_lat = (_rms(proj[:, o:o + LORA]) * gq_ref[...]).astype(BF16)
    q = jnp.dot(q_lat, wq_ref[...], preferred_element_type=F32) * (MLA_QK_DIM ** -0.5)
    for hd in range(MLA_HEADS):
        c0 = hd * 2 * HEAD_DIM
        mq_ref[:, c0:c0 + HEAD_DIM] = q[:, c0:c0 + HEAD_DIM].astype(BF16)
        mq_ref[:, c0 + HEAD_DIM:c0 + 2 * HEAD_DIM] = rope(q[:, c0 + HEAD_DIM:c0 + 2 * HEAD_DIM]).astype(BF16)

    kv_lat = (_rms(proj[:, o + LORA:o + 2 * LORA]) * gkv_ref[...]).astype(BF16)
    kv = jnp.dot(kv_lat, wkv_ref[...], preferred_element_type=F32).astype(BF16)
    kr = proj[:, o + 2 * LORA:o + 2 * LORA + ROPE_DIM]
    k_rope = rope(jnp.concatenate([kr, jnp.zeros((tm, LANES - ROPE_DIM), F32)], axis=1)).astype(BF16)
    kr32 = k_rope.astype(F32)
    rope_sq = jnp.sum(kr32 * kr32, axis=1, keepdims=True)
    key_sq = []
    for hd in range(MLA_HEADS):
        c0 = hd * MLA_KV_STRIDE
        k_nope = kv[:, hd * 2 * HEAD_DIM:hd * 2 * HEAD_DIM + HEAD_DIM]
        mkv_ref[:, c0:c0 + HEAD_DIM] = k_nope
        mkv_ref[:, c0 + HEAD_DIM:c0 + 2 * HEAD_DIM] = k_rope
        mkv_ref[:, c0 + 2 * HEAD_DIM:c0 + 3 * HEAD_DIM] = kv[:, hd * 2 * HEAD_DIM + HEAD_DIM:(hd + 1) * 2 * HEAD_DIM]
        kn32 = k_nope.astype(F32)
        sq = jnp.sum(kn32 * kn32, axis=1, keepdims=True) + rope_sq
        key_sq.append(jnp.broadcast_to(jnp.max(sq, axis=0, keepdims=True), (1, LANES)))
    ksq_ref[0] = jnp.concatenate(key_sq, axis=0)


def _inproj(x2, sc1, sh1, g_attn, pos, rope_tables, w_all, g_q, wq, g_kv, wkv, seq):
    n, d = x2.shape
    tm = ROW_TILE
    wcols = w_all.shape[1]
    per_b = seq // tm
    row = lambda i: (i, 0)
    bat = lambda i: (i // per_b, 0, 0)
    outs = [
        jax.ShapeDtypeStruct((n, SB_WIDTH), BF16),
        jax.ShapeDtypeStruct((n, SB_WIDTH), BF16),
        jax.ShapeDtypeStruct((n, SB_WIDTH), BF16),
        jax.ShapeDtypeStruct((n, 2 * MLA_WIDTH), BF16),
        jax.ShapeDtypeStruct((n, MLA_HEADS * MLA_KV_STRIDE), BF16),
        jax.ShapeDtypeStruct((n // tm, MLA_HEADS, LANES), F32),
    ]
    return pl.pallas_call(
        _inproj_kernel,
        out_shape=outs,
        grid=(n // tm,),
        in_specs=[
            pl.BlockSpec((tm, d), row),
            pl.BlockSpec((1, 1, d), bat),
            pl.BlockSpec((1, 1, d), bat),
            _resident((1, d)),
            pl.BlockSpec((1, tm), lambda i: (0, i)),
            _resident((1, LANES)),
            _resident((1, LANES)),
            _resident((1, LANES)),
            _resident((1, LANES)),
            _resident((d, wcols)),
            _resident((1, LORA)),
            _resident((LORA, 2 * MLA_WIDTH)),
            _resident((1, LORA)),
            _resident((LORA, 2 * MLA_WIDTH)),
        ],
        out_specs=[
            pl.BlockSpec((tm, SB_WIDTH), row),
            pl.BlockSpec((tm, SB_WIDTH), row),
            pl.BlockSpec((tm, SB_WIDTH), row),
            pl.BlockSpec((tm, 2 * MLA_WIDTH), row),
            pl.BlockSpec((tm, MLA_HEADS * MLA_KV_STRIDE), row),
            pl.BlockSpec((1, MLA_HEADS, LANES), lambda i: (i, 0, 0)),
        ],
        compiler_params=_cparams(("arbitrary",)),
        name="inproj",
    )(x2, sc1, sh1, g_attn, pos, *rope_tables, w_all, g_q, wq, g_kv, wkv)


_NT = (((1,), (1,)), ((), ()))


def _sb_kernel(q_ref, k_ref, v_ref, o_ref, *, tile, heads):
    i = pl.program_id(2)
    row = lax.broadcasted_iota(I32, (tile, tile), 0)
    col = lax.broadcasted_iota(I32, (tile, tile), 1)
    suffix = jnp.where(row >= col, 1.0, 0.0).astype(BF16)
    strict = col < row

    def head_step(h, kt, c, acc, diag):
        hs = slice(h * HEAD_DIM, (h + 1) * HEAD_DIM)
        k0 = pl.multiple_of(kt * tile, tile)
        k = k_ref[pl.ds(k0, tile), hs]
        v = v_ref[pl.ds(k0, tile), hs]
        z = lax.dot_general(q_ref[:, hs], k, _NT, preferred_element_type=F32)
        log_beta = jnp.minimum(z, 0.0) - jnp.log(1.0 + jnp.exp(-jnp.abs(z)))
        log_keep = log_beta - z
        if diag:
            log_keep = jnp.where(strict, log_keep, 0.0)
        incl = jnp.dot(log_keep.astype(BF16), suffix, preferred_element_type=F32)
        a = jnp.exp(incl + z + c)
        if diag:
            a = jnp.where(strict, a, 0.0)
        acc = acc + jnp.dot(a.astype(BF16), v, preferred_element_type=F32)
        return c + incl[:, 0:1], acc

    def step(kt, carry, diag):
        return tuple(head_step(h, kt, *carry[h], diag) for h in range(heads))

    def live(carry):
        c_max = carry[0][0]
        for h in range(1, heads):
            c_max = jnp.maximum(c_max, carry[h][0])
        return (jnp.max(c_max) > SB_DEAD_LOG).astype(I32)

    zero = ((jnp.zeros((tile, 1), F32), jnp.zeros((tile, HEAD_DIM), F32)),) * heads

    def first_two():
        return step(i - 1, step(i, zero, True), False)

    carry = lax.cond(i > 0, first_two, lambda: step(i, zero, True))

    def body(state):
        kt, _, cr = state
        cr = step(kt, cr, False)
        return kt - 1, live(cr), cr

    _, _, carry = lax.while_loop(lambda st: (st[0] >= 0) & (st[1] > 0), body, (i - 2, live(carry), carry))
    for h in range(heads):
        o_ref[:, h * HEAD_DIM:(h + 1) * HEAD_DIM] = carry[h][1].astype(BF16)


def _sb_attention(sbq, sbk, sbv, batch, seq):
    n = sbq.shape[0]
    t = ATTN_TILE
    hb = ATTN_HEADS_PER_STEP
    nq = seq // t
    qmap = lambda b, h, i: (b * nq + i, h)
    kmap = lambda b, h, i: (b, h)
    return pl.pallas_call(
        functools.partial(_sb_kernel, tile=t, heads=hb),
        out_shape=jax.ShapeDtypeStruct((n, SB_WIDTH), BF16),
        grid=(batch, SB_HEADS // hb, nq),
        in_specs=[
            pl.BlockSpec((t, hb * HEAD_DIM), qmap),
            pl.BlockSpec((seq, hb * HEAD_DIM), kmap),
            pl.BlockSpec((seq, hb * HEAD_DIM), kmap),
        ],
        out_specs=pl.BlockSpec((t, hb * HEAD_DIM), qmap),
        compiler_params=_cparams(("arbitrary", "arbitrary", "arbitrary")),
        name="sb_attn",
    )(sbq, sbk, sbv)


def _mla_kernel(q_ref, kv_ref, kn_ref, o_ref, *, tq, tk, heads):
    i = pl.program_id(2)
    row = lax.broadcasted_iota(I32, (tq, tk), 0)
    col = lax.broadcasted_iota(I32, (tq, tk), 1)
    qk = 2 * HEAD_DIM
    per_q = tq // tk

    def scores(h, kt, diag):
        k0 = pl.multiple_of(kt * tk, tk)
        k = kv_ref[pl.ds(k0, tk), h * MLA_KV_STRIDE:h * MLA_KV_STRIDE + qk]
        s = lax.dot_general(q_ref[:, h * qk:(h + 1) * qk], k, _NT, preferred_element_type=F32)
        if diag:
            s = jnp.where(col + kt * tk <= row + i * tq, s, NEG_BIG)
        return s

    def lane_fold(x, op):
        out = x[:, :LANES]
        for cblk in range(1, tk // LANES):
            out = op(out, x[:, cblk * LANES:(cblk + 1) * LANES])
        return out

    def over_keys(step, carry):
        for r in range(per_q):
            carry = step(i * per_q + r, carry, True)
        return lax.fori_loop(0, i * per_q, lambda s, cr: step(s, cr, False), carry)

    def softmax_pv(shift):
        def acc_step(kt, carry, diag):
            out = []
            for h in range(heads):
                l, acc = carry[h]
                k0 = pl.multiple_of(kt * tk, tk)
                v = kv_ref[pl.ds(k0, tk), h * MLA_KV_STRIDE + qk:(h + 1) * MLA_KV_STRIDE]
                s = scores(h, kt, diag)
                p = jnp.concatenate([jnp.exp(s[:, c * LANES:(c + 1) * LANES] - shift[h])
                                     for c in range(tk // LANES)], axis=1)
                out.append((l + lane_fold(p, jnp.add),
                            acc + jnp.dot(p.astype(BF16), v, preferred_element_type=F32)))
            return tuple(out)

        return over_keys(acc_step, ((jnp.zeros((tq, LANES), F32), jnp.zeros((tq, HEAD_DIM), F32)),) * heads)

    def row_sums(carry):
        return tuple(jnp.sum(carry[h][0], axis=1, keepdims=True) for h in range(heads))

    def write(carry, sums):
        for h in range(heads):
            o_ref[:, h * HEAD_DIM:(h + 1) * HEAD_DIM] = (carry[h][1] / sums[h]).astype(BF16)

    head0 = pl.program_id(1) * heads
    bound = []
    for h in range(heads):
        qh = q_ref[:, h * qk:(h + 1) * qk].astype(F32)
        q_norm = jnp.sqrt(jnp.sum(qh * qh, axis=1, keepdims=True))
        bound.append(q_norm * kn_ref[0, pl.ds(head0 + h, 1), :] * MLA_BOUND_SLACK)
    fast = softmax_pv(tuple(bound))
    fast_sums = row_sums(fast)
    smallest = fast_sums[0]
    for h in range(1, heads):
        smallest = jnp.minimum(smallest, fast_sums[h])
    in_range = jnp.min(smallest) > MLA_MIN_ROW_SUM
    write(fast, fast_sums)

    @pl.when(jnp.logical_not(in_range))
    def _():
        def max_step(kt, ms, diag):
            return tuple(jnp.maximum(ms[h], lane_fold(scores(h, kt, diag), jnp.maximum)) for h in range(heads))

        ms = over_keys(max_step, (jnp.full((tq, LANES), NEG_BIG, F32),) * heads)
        row_max = tuple(jnp.broadcast_to(jnp.max(ms[h], axis=1, keepdims=True), (tq, LANES))
                        for h in range(heads))
        exact = softmax_pv(row_max)
        write(exact, row_sums(exact))


def _mla_attention(mq, mkv, k_norm, batch, seq):
    n = mq.shape[0]
    t = MLA_Q_TILE
    hb = MLA_HEADS_PER_STEP
    nq = seq // t
    return pl.pallas_call(
        functools.partial(_mla_kernel, tq=t, tk=MLA_K_TILE, heads=hb),
        out_shape=jax.ShapeDtypeStruct((n, MLA_WIDTH), BF16),
        grid=(batch, MLA_HEADS // hb, nq),
        in_specs=[
            pl.BlockSpec((t, hb * 2 * HEAD_DIM), lambda b, h, i: (b * nq + i, h)),
            pl.BlockSpec((seq, hb * MLA_KV_STRIDE), lambda b, h, i: (b, h), pipeline_mode=pl.Buffered(1)),
            pl.BlockSpec((1, MLA_HEADS, LANES), lambda b, h, i: (b, 0, 0)),
        ],
        out_specs=pl.BlockSpec((t, hb * HEAD_DIM), lambda b, h, i: (b * nq + i, h)),
        compiler_params=_cparams(("arbitrary", "arbitrary", "arbitrary"), MLA_VMEM_LIMIT_BYTES),
        name="mla_attn",
    )(mq, mkv, k_norm)


def _outproj_kernel(sb_ref, mla_ref, x_ref, g1_ref, sc_ref, sh_ref, g_ref, wo_ref, wr_ref, br_ref,
                    x1_ref, hp_ref, ii_ref, iw_ref, cnt_ref, run_ref):
    step = pl.program_id(0)
    tm, d = x_ref.shape

    @pl.when(step == 0)
    def _():
        run_ref[...] = jnp.zeros_like(run_ref)

    o = (jnp.dot(sb_ref[...], wo_ref[:SB_WIDTH, :], preferred_element_type=F32)
         + jnp.dot(mla_ref[...], wo_ref[SB_WIDTH:, :], preferred_element_type=F32))
    x1 = x_ref[...] + g1_ref[0] * o
    x1_ref[...] = x1
    h2 = _rms(x1) * g_ref[...] * sc_ref[0] + sh_ref[0]
    hb = h2.astype(BF16)
    hb32 = hb.astype(F32)
    hp_ref[...] = hb32

    h_lo = (h2 - hb32).astype(BF16)
    w = wr_ref[...]
    w_hi = w.astype(BF16)
    w_lo = (w - w_hi.astype(F32)).astype(BF16)
    logits = (lax.dot_general(w_hi, hb, _NT, preferred_element_type=F32)
              + lax.dot_general(w_hi, h_lo, _NT, preferred_element_type=F32)
              + lax.dot_general(w_lo, hb, _NT, preferred_element_type=F32)) + br_ref[...]

    expert = lax.broadcasted_iota(I32, (N_EXPERTS, tm), 0)
    work = logits
    vals, idxs, hots = [], [], []
    for _ in range(TOP_K):
        m = jnp.max(work, axis=0, keepdims=True)
        idx = jnp.min(jnp.where(work == m, expert, N_EXPERTS), axis=0, keepdims=True)
        hot = expert == idx
        work = jnp.where(hot, -jnp.inf, work)
        vals.append(m)
        idxs.append(idx)
        hots.append(hot)
    exps = [jnp.exp(v - vals[0]) for v in vals]
    den = exps[0] + exps[1] + exps[2] + exps[3]
    sel = jnp.where(hots[0] | hots[1] | hots[2] | hots[3], 1.0, 0.0)

    r = lax.broadcasted_iota(I32, (tm, tm), 0)
    c = lax.broadcasted_iota(I32, (tm, tm), 1)
    earlier = jnp.where(r < c, 1.0, 0.0).astype(BF16)
    rank = jnp.dot(sel.astype(BF16), earlier, preferred_element_type=F32) + run_ref[...]
    run_ref[...] = run_ref[...] + jnp.sum(sel, axis=1, keepdims=True)
    cnt_ref[...] = run_ref[...]

    field = lax.broadcasted_iota(I32, (2 * TOP_K, tm), 0)
    info_i = jnp.zeros((2 * TOP_K, tm), I32)
    info_w = jnp.zeros((2 * TOP_K, tm), F32)
    for k in range(TOP_K):
        rank_k = jnp.sum(jnp.where(hots[k], rank, 0.0), axis=0, keepdims=True).astype(I32)
        info_i = jnp.where(field == k, idxs[k], info_i)
        info_i = jnp.where(field == TOP_K + k, rank_k, info_i)
        info_w = jnp.where(field == k, exps[k] / den, info_w)
    ii_ref[...] = info_i
    iw_ref[...] = info_w


def _outproj(sb, mla, x2, g1, sc2, sh2, g_ffn, wo, w_router, b_router, seq):
    n, d = x2.shape
    tm = OUTPROJ_TILE
    per_b = seq // tm
    row = lambda i: (i, 0)
    bat = lambda i: (i // per_b, 0, 0)
    outs = [
        jax.ShapeDtypeStruct((n, d), F32),
        jax.ShapeDtypeStruct((n, d), F32),
        jax.ShapeDtypeStruct((2 * TOP_K, n), I32),
        jax.ShapeDtypeStruct((2 * TOP_K, n), F32),
        jax.ShapeDtypeStruct((N_EXPERTS, 1), F32),
    ]
    return pl.pallas_call(
        _outproj_kernel,
        out_shape=outs,
        grid=(n // tm,),
        in_specs=[
            pl.BlockSpec((tm, SB_WIDTH), row),
            pl.BlockSpec((tm, MLA_WIDTH), row),
            pl.BlockSpec((tm, d), row),
            pl.BlockSpec((1, 1, d), bat),
            pl.BlockSpec((1, 1, d), bat),
            pl.BlockSpec((1, 1, d), bat),
            _resident((1, d)),
            _resident((SB_WIDTH + MLA_WIDTH, d)),
            _resident((N_EXPERTS, d)),
            _resident((N_EXPERTS, 1)),
        ],
        out_specs=[
            pl.BlockSpec((tm, d), row),
            pl.BlockSpec((tm, d), row),
            pl.BlockSpec((2 * TOP_K, tm), lambda i: (0, i)),
            pl.BlockSpec((2 * TOP_K, tm), lambda i: (0, i)),
            pl.BlockSpec((N_EXPERTS, 1), lambda i: (0, 0)),
        ],
        scratch_shapes=[pltpu.VMEM((N_EXPERTS, 1), F32)],
        compiler_params=_cparams(("arbitrary",)),
        name="outproj",
    )(sb, mla, x2, g1, sc2, sh2, g_ffn, wo, w_router, b_router)


def _row_copy(src_ref, src_row, dst_ref, dst_row, sem):
    return pltpu.make_async_copy(src_ref.at[pl.ds(src_row, 1), :], dst_ref.at[pl.ds(dst_row, 1), :], sem)


def _dispatch_kernel(pad_lo_ref, pad_hi_ref, dest_ref, h_ref, xs_ref, zero_ref, sem):
    tm = h_ref.shape[0]

    @pl.when(pl.program_id(0) == 0)
    def _():
        zero_ref[...] = jnp.zeros_like(zero_ref)

        def put(r, c):
            _row_copy(zero_ref, 0, xs_ref, r, sem).start()
            return c

        def got(r, c):
            _row_copy(zero_ref, 0, xs_ref, 0, sem).wait()
            return c

        def over_pad_rows(fn):
            def per_expert(e, carry):
                return lax.fori_loop(pad_lo_ref[e], pad_hi_ref[e], fn, carry)

            lax.fori_loop(0, N_EXPERTS, per_expert, 0)

        over_pad_rows(put)
        over_pad_rows(got)

    def issue(r, carry):
        for k in range(TOP_K):
            _row_copy(h_ref, r, xs_ref, dest_ref[r * TOP_K + k], sem).start(priority=k % 2)
        return carry

    lax.fori_loop(0, tm, issue, 0)
    for k in range(TOP_K):
        pltpu.make_async_copy(h_ref, xs_ref.at[pl.ds(0, tm), :], sem).wait()


def _dispatch(pad_lo, pad_hi, dest_flat, hp, rows):
    n, w = hp.shape
    tm = ROW_TILE
    grid_spec = pltpu.PrefetchScalarGridSpec(
        num_scalar_prefetch=2,
        grid=(n // tm,),
        in_specs=[
            pl.BlockSpec((tm * TOP_K,), lambda i, lo, hi: (i,), memory_space=pltpu.SMEM),
            pl.BlockSpec((tm, w), lambda i, lo, hi: (i, 0)),
        ],
        out_specs=pl.BlockSpec(memory_space=pl.ANY),
        scratch_shapes=[pltpu.VMEM((8, w), F32), pltpu.SemaphoreType.DMA(())],
    )
    return pl.pallas_call(
        _dispatch_kernel,
        out_shape=jax.ShapeDtypeStruct((rows, w), F32),
        grid_spec=grid_spec,
        compiler_params=_cparams(("arbitrary",)),
        name="dispatch",
    )(pad_lo, pad_hi, dest_flat, hp)


def _moe_kernel(ie_ref, ix_ref, ic_ref, iv_ref, ns_ref, xs_ref, w1_ref, b1g_ref, b1l_ref, w2_ref, b2_ref,
                p_ref, ys_ref, xb_ref, gl_ref, w2b_ref):
    del ie_ref, ix_ref
    item = pl.program_id(0)
    nj, fc, _ = w2b_ref.shape
    chunk = ic_ref[item]

    def ffn(j, rows):
        xt = xb_ref[:rows, :]
        ug = jnp.dot(xt, gl_ref[j, :, :fc], preferred_element_type=F32) + b1g_ref[j]
        ul = jnp.dot(xt, gl_ref[j, :, fc:], preferred_element_type=F32) + b1l_ref[j]
        glu = jnp.minimum(ug, SWIGLU_LIMIT)
        lin = jnp.clip(ul, -SWIGLU_LIMIT, SWIGLU_LIMIT)
        act = glu * jax.nn.sigmoid(SWIGLU_ALPHA * glu) * (lin + 1.0)
        ys_ref[:rows, :] += jnp.dot(act.astype(BF16), w2b_ref[j], preferred_element_type=F32)

    def for_used_rows(fn):
        for n_sub in range(1, MOE_TILE // MOE_SUB + 1):
            pl.when(ns_ref[item] == n_sub)(functools.partial(fn, n_sub * MOE_SUB))

    @pl.when(iv_ref[item] == 1)
    def _():
        @pl.when((chunk == 0) | (chunk == nj))
        def _():
            def load_rows(rows):
                xb_ref[:rows, :] = xs_ref[:rows, :].astype(BF16)

            for_used_rows(load_rows)
            ys_ref[...] = jnp.broadcast_to(b2_ref[...], ys_ref.shape)

        @pl.when(chunk < nj)
        def _():
            p = p_ref[...]
            for blk in range(2 * fc // 256):
                wb = w1_ref[:, blk * 256:(blk + 1) * 256].astype(BF16)
                split = jnp.dot(wb, p, preferred_element_type=F32).astype(BF16)
                gl_ref[chunk, :, blk * LANES:(blk + 1) * LANES] = split[:, :LANES]
                gl_ref[chunk, :, fc + blk * LANES:fc + (blk + 1) * LANES] = split[:, LANES:]
            w2b_ref[chunk] = w2_ref[...].astype(BF16)
            for_used_rows(functools.partial(ffn, chunk))

        @pl.when(chunk == nj)
        def _():
            def all_chunks(rows):
                def body(j, carry):
                    ffn(j, rows)
                    return carry

                lax.fori_loop(0, nj, body, 0, unroll=4)

            for_used_rows(all_chunks)


def _lookup(table, idx):
    hit = idx[..., None] == jnp.arange(table.shape[0], dtype=idx.dtype)
    return jnp.sum(jnp.where(hit, table, 0), axis=-1)


def _moe_items(tile_e, tile_v, tile_f, tile_ns, nj):
    n_tiles = tile_v.shape[0]
    n_items = n_tiles + N_EXPERTS * (nj - 1)
    cost = jnp.where(tile_f == 1, nj, 1) * tile_v
    first_item = jnp.cumsum(cost) - cost
    total = jnp.sum(cost)
    last_tile = jnp.maximum(jnp.sum(tile_v) - 1, 0)
    item = jnp.arange(n_items, dtype=I32)
    valid = item < total
    tile = jnp.where(valid, jnp.sum(first_item[None, :] <= item[:, None], axis=1) - 1, last_tile).astype(I32)
    chunk = jnp.where(valid & (_lookup(tile_f, tile) == 1), item - _lookup(first_item, tile), nj).astype(I32)
    return _lookup(tile_e, tile), tile, chunk, valid.astype(I32), (_lookup(tile_ns, tile) * valid).astype(I32)


def _moe(item_e, item_x, item_c, item_v, item_ns, xs, w1, b1g, b1l, w2, b2, perm):
    rows, d = xs.shape
    f = w1.shape[2] // 2
    fc = MOE_FC
    nj = f // fc

    def wj(q, ic):
        return jnp.minimum(ic[q], nj - 1)

    grid_spec = pltpu.PrefetchScalarGridSpec(
        num_scalar_prefetch=5,
        grid=(item_e.shape[0],),
        in_specs=[
            pl.BlockSpec((MOE_TILE, d), lambda q, ie, ix, ic, iv, ns: (ix[q], 0)),
            pl.BlockSpec((None, d, 2 * fc), lambda q, ie, ix, ic, iv, ns: (ie[q], 0, wj(q, ic))),
            pl.BlockSpec((None, nj, 1, fc), lambda q, ie, ix, ic, iv, ns: (ie[q], 0, 0, 0)),
            pl.BlockSpec((None, nj, 1, fc), lambda q, ie, ix, ic, iv, ns: (ie[q], 0, 0, 0)),
            pl.BlockSpec((None, fc, d), lambda q, ie, ix, ic, iv, ns: (ie[q], wj(q, ic), 0)),
            pl.BlockSpec((None, 1, d), lambda q, ie, ix, ic, iv, ns: (ie[q], 0, 0)),
            pl.BlockSpec((256, 256), lambda q, ie, ix, ic, iv, ns: (0, 0)),
        ],
        out_specs=pl.BlockSpec((MOE_TILE, d), lambda q, ie, ix, ic, iv, ns: (ix[q], 0)),
        scratch_shapes=[
            pltpu.VMEM((MOE_TILE, d), BF16),
            pltpu.VMEM((nj, d, 2 * fc), BF16),
            pltpu.VMEM((nj, fc, d), BF16),
        ],
    )
    return pl.pallas_call(
        _moe_kernel,
        out_shape=jax.ShapeDtypeStruct((rows, d), F32),
        grid_spec=grid_spec,
        compiler_params=_cparams(("arbitrary",)),
        name="moe",
    )(item_e, item_x, item_c, item_v, item_ns, xs, w1, b1g, b1l, w2, b2, perm)


def _combine_kernel(dest_ref, next_ref, w_ref, x1_ref, g2_ref, gf_ref, ys_ref, o_ref, buf_ref, sem):
    i = pl.program_id(0)
    tm = x1_ref.shape[0]

    def gather(idx_ref, s):
        def issue(r, carry):
            for k in range(TOP_K):
                _row_copy(ys_ref, idx_ref[r * TOP_K + k], buf_ref.at[s], k * tm + r, sem.at[s]).start(priority=k % 2)
            return carry

        lax.fori_loop(0, tm, issue, 0)

    @pl.when(i == 0)
    def _():
        gather(dest_ref, 0)

    def reduce_slot(s):
        @pl.when(i + 1 < pl.num_programs(0))
        def _():
            gather(next_ref, 1 - s)

        pltpu.make_async_copy(ys_ref.at[pl.ds(0, TOP_K * tm), :], buf_ref.at[s], sem.at[s]).wait()
        w = w_ref[...]
        moe = w[:, 0:1] * buf_ref[s, :tm, :]
        for k in range(1, TOP_K):
            moe = moe + w[:, k:k + 1] * buf_ref[s, k * tm:(k + 1) * tm, :]
        x = x1_ref[...] + g2_ref[0] * moe
        o_ref[...] = _rms(x) * gf_ref[...]

    for s in range(2):
        pl.when(i % 2 == s)(functools.partial(reduce_slot, s))


def _combine(dest_flat, weights, x1, g2, g_final, ys, seq):
    n, d = x1.shape
    tm = COMBINE_TILE
    per_b = seq // tm
    steps = n // tm
    return pl.pallas_call(
        _combine_kernel,
        out_shape=jax.ShapeDtypeStruct((n, d), F32),
        grid=(steps,),
        in_specs=[
            pl.BlockSpec((tm * TOP_K,), lambda i: (i,), memory_space=pltpu.SMEM),
            pl.BlockSpec((tm * TOP_K,), lambda i: (jnp.minimum(i + 1, steps - 1),), memory_space=pltpu.SMEM),
            pl.BlockSpec((tm, TOP_K), lambda i: (i, 0)),
            pl.BlockSpec((tm, d), lambda i: (i, 0)),
            pl.BlockSpec((1, 1, d), lambda i: (i // per_b, 0, 0)),
            _resident((1, d)),
            pl.BlockSpec(memory_space=pl.ANY),
        ],
        out_specs=pl.BlockSpec((tm, d), lambda i: (i, 0)),
        scratch_shapes=[pltpu.VMEM((2, TOP_K * tm, d), F32), pltpu.SemaphoreType.DMA((2,))],
        compiler_params=_cparams(("arbitrary",)),
        name="combine",
    )(dest_flat, dest_flat, weights, x1, g2, g_final, ys)


def _rope_lane_tables():
    inv_freq = 1.0 / (ROPE_THETA ** (jnp.arange(ROPE_HALF, dtype=F32) * 2.0 / ROPE_DIM))
    z = jnp.zeros((ROPE_HALF,), F32)
    o = jnp.ones((ROPE_HALF,), F32)
    invf = jnp.concatenate([inv_freq, inv_freq, z, z])[None, :]
    cmask = jnp.concatenate([o, o, z, z])[None, :]
    smask_x2 = jnp.concatenate([-o, z, z, z])[None, :]
    smask_x1 = jnp.concatenate([z, o, z, z])[None, :]
    return invf, cmask, smask_x2, smask_x1


def _pad_rope_cols(w):
    return jnp.concatenate([w, jnp.zeros(w.shape[:-1] + (LANES - ROPE_DIM,), w.dtype)], axis=-1)


def kernel(x, c, positions, g_attn, w_mod, b_mod, w_in, g_q_lat, w_q_up, g_kv_lat, w_kv_up, w_out, g_ffn,
           w_router, b_router, w1, b1, w2, b2, g_final):
    batch, seq, d = x.shape
    n = batch * seq
    depth = w_mod.shape[0]
    assert depth == 1 and seq % ATTN_TILE == 0 and seq % ROW_TILE == 0 and d == SB_WIDTH + MLA_WIDTH
    x2 = x.reshape(n, d)

    c_pad = jnp.zeros((8, d), F32).at[:batch].set(c)
    mod = _mod(c_pad, w_mod[0], b_mod[0][None, :])[:batch]
    shift1, scale1, gate1, shift2, scale2, gate2 = [m[:, None, :] for m in jnp.split(mod, 6, axis=-1)]

    w_all = w_in[0].astype(BF16)
    wq = w_q_up[0].reshape(LORA, MLA_HEADS, MLA_QK_DIM)
    wq = jnp.concatenate([wq[..., :HEAD_DIM], _pad_rope_cols(wq[..., HEAD_DIM:])], axis=-1)
    wq = wq.reshape(LORA, MLA_HEADS * 2 * HEAD_DIM).astype(BF16)
    wkv = w_kv_up[0].astype(BF16)

    sbq, sbk, sbv, mq, mkv, key_sq = _inproj(
        x2, 1.0 + scale1, shift1, g_attn, positions.reshape(1, n), _rope_lane_tables(), w_all,
        g_q_lat, wq, g_kv_lat, wkv, seq)

    sb_out = _sb_attention(sbq, sbk, sbv, batch, seq)
    k_norm = jnp.sqrt(jnp.max(key_sq.reshape(batch, -1, MLA_HEADS, LANES), axis=1))
    mla_out = _mla_attention(mq, mkv, k_norm, batch, seq)

    x1, hp, info_i, info_w, counts = _outproj(
        sb_out, mla_out, x2, gate1, 1.0 + scale2, shift2, g_ffn, w_out[0].astype(BF16),
        w_router[0].T, b_router.reshape(N_EXPERTS, 1), seq)

    counts = counts[:, 0].astype(I32)
    padded = (counts + MOE_TILE - 1) // MOE_TILE * MOE_TILE
    pend = jnp.cumsum(padded)
    pstart = pend - padded
    idx = info_i[:TOP_K].T
    dest = (_lookup(pstart, idx) + info_i[TOP_K:].T).reshape(n * TOP_K)
    rows = n * TOP_K + N_EXPERTS * MOE_TILE
    n_tiles = rows // MOE_TILE
    tile_row0 = jnp.arange(n_tiles, dtype=I32) * MOE_TILE
    tile_v = (tile_row0 < pend[-1]).astype(I32)
    last = jnp.maximum(pend[-1] // MOE_TILE - 1, 0)
    tile_x = jnp.minimum(jnp.arange(n_tiles, dtype=I32), last)
    tile_e = jnp.minimum(jnp.sum(pend[None, :] <= (tile_x * MOE_TILE)[:, None], axis=1), N_EXPERTS - 1).astype(I32)
    tile_start = _lookup(pstart, tile_e)
    tile_f = (tile_x * MOE_TILE == tile_start).astype(I32) * tile_v
    used = _lookup(counts, tile_e) - (tile_x * MOE_TILE - tile_start)
    tile_ns = jnp.clip((used + MOE_SUB - 1) // MOE_SUB, 0, MOE_TILE // MOE_SUB) * tile_v

    xs = _dispatch(pstart + counts, pstart + (counts + MOE_SUB - 1) // MOE_SUB * MOE_SUB, dest, hp, rows)

    src = jnp.arange(256)
    perm = (jnp.arange(256)[None, :] == (src % 2 * LANES + src // 2)[:, None]).astype(BF16)
    f = w1.shape[-1] // 2
    nj = f // MOE_FC
    b1r = b1[0].reshape(N_EXPERTS, nj, 1, MOE_FC, 2)
    items = _moe_items(tile_e, tile_v, tile_f, tile_ns, nj)
    ys = _moe(*items, xs, w1[0], b1r[..., 0], b1r[..., 1], w2[0], b2[0][:, None, :], perm)

    out = _combine(dest, info_w[:TOP_K].T, x1, gate2, g_final[None, :], ys, seq)
    return out.reshape(batch, seq, d)
```

```python
import functools

import jax
import jax.numpy as jnp
from jax import lax
from jax.experimental import pallas as pl
from jax.experimental.pallas import tpu as pltpu

F32 = jnp.float32
BF16 = jnp.bfloat16
I32 = jnp.int32

SB_HEADS = 8
MLA_HEADS = 8
HEAD_DIM = 128
ROPE_DIM = 64
ROPE_HALF = ROPE_DIM // 2
MLA_QK_DIM = HEAD_DIM + ROPE_DIM
LORA = 512
ROPE_THETA = 10000.0
N_EXPERTS = 32
TOP_K = 4
SWIGLU_ALPHA = 1.702
SWIGLU_LIMIT = 7.0
NORM_EPS = 1e-6
SB_WIDTH = SB_HEADS * HEAD_DIM
MLA_WIDTH = MLA_HEADS * HEAD_DIM

LANES = 128
VMEM_LIMIT_BYTES = 56 * 1024 * 1024
MLA_VMEM_LIMIT_BYTES = 60 * 1024 * 1024

ROW_TILE = 256
OUTPROJ_TILE = 512
ATTN_TILE = 256
MLA_Q_TILE = 512
MLA_K_TILE = 512
MLA_BOUND_SLACK = 1.01
MLA_MIN_ROW_SUM = 1e-30
ATTN_HEADS_PER_STEP = 4
MLA_HEADS_PER_STEP = 8
MLA_KV_STRIDE = 3 * HEAD_DIM
SB_DEAD_LOG = -104.0
MOE_TILE = 512
MOE_SUB = 256
MOE_FC = 256
COMBINE_TILE = 256
NEG_BIG = -1e30


def _cparams(sem, vmem_limit_bytes=VMEM_LIMIT_BYTES):
    return pltpu.CompilerParams(dimension_semantics=sem, vmem_limit_bytes=vmem_limit_bytes)


def _resident(shape):
    nd = len(shape)
    return pl.BlockSpec(shape, lambda *_: (0,) * nd, pipeline_mode=pl.Buffered(1))


def _rms(v):
    return v * lax.rsqrt(jnp.mean(v * v, axis=-1, keepdims=True) + NORM_EPS)


def _mod_kernel(c_ref, w_ref, b_ref, o_ref):
    c = c_ref[...]
    a = (c * jax.nn.sigmoid(c)).astype(BF16)
    o_ref[...] = jnp.dot(a, w_ref[...].astype(BF16), preferred_element_type=F32) + b_ref[...]


def _mod(c_pad, w_mod, b_mod):
    rows, d = c_pad.shape
    n_out = w_mod.shape[1]
    tn = 1024
    return pl.pallas_call(
        _mod_kernel,
        out_shape=jax.ShapeDtypeStruct((rows, n_out), F32),
        grid=(n_out // tn,),
        in_specs=[
            pl.BlockSpec((rows, d), lambda j: (0, 0)),
            pl.BlockSpec((d, tn), lambda j: (0, j)),
            pl.BlockSpec((1, tn), lambda j: (0, j)),
        ],
        out_specs=pl.BlockSpec((rows, tn), lambda j: (0, j)),
        compiler_params=_cparams(("arbitrary",)),
        name="mod",
    )(c_pad, w_mod, b_mod)


def _inproj_kernel(x_ref, sc_ref, sh_ref, g_ref, pos_ref, invf_ref, cm_ref, sm2_ref, sm1_ref, w_ref,
                   gq_ref, wq_ref, gkv_ref, wkv_ref,
                   sbq_ref, sbk_ref, sbv_ref, mq_ref, mkv_ref, ksq_ref):
    h = (_rms(x_ref[...]) * g_ref[...] * sc_ref[0] + sh_ref[0]).astype(BF16)
    proj = jnp.dot(h, w_ref[...], preferred_element_type=F32)

    sbq_ref[...] = (proj[:, :SB_WIDTH] * (HEAD_DIM ** -0.5)).astype(BF16)
    sbk_ref[...] = proj[:, SB_WIDTH:2 * SB_WIDTH].astype(BF16)
    sbv_ref[...] = proj[:, 2 * SB_WIDTH:3 * SB_WIDTH].astype(BF16)

    tm = x_ref.shape[0]
    pos = jnp.broadcast_to(pos_ref[...].astype(F32), (LANES, tm))
    ang = pos.T * invf_ref[...]
    cosm = jnp.cos(ang) * cm_ref[...]
    sin = jnp.sin(ang)
    sin_x2 = sin * sm2_ref[...]
    sin_x1 = sin * sm1_ref[...]

    def rope(v):
        return (v * cosm + pltpu.roll(v, LANES - ROPE_HALF, 1) * sin_x2
                + pltpu.roll(v, ROPE_HALF, 1) * sin_x1)

    o = 3 * SB_WIDTH
    q_lat = (_rms(proj[:, o:o + LORA]) * gq_ref[...]).astype(BF16)
    q = jnp.dot(q_lat, wq_ref[...], preferred_element_type=F32) * (MLA_QK_DIM ** -0.5)
    for hd in range(MLA_HEADS):
        c0 = hd * 2 * HEAD_DIM
        mq_ref[:, c0:c0 + HEAD_DIM] = q[:, c0:c0 + HEAD_DIM].astype(BF16)
        mq_ref[:, c0 + HEAD_DIM:c0 + 2 * HEAD_DIM] = rope(q[:, c0 + HEAD_DIM:c0 + 2 * HEAD_DIM]).astype(BF16)

    kv_lat = (_rms(proj[:, o + LORA:o + 2 * LORA]) * gkv_ref[...]).astype(BF16)
    kv = jnp.dot(kv_lat, wkv_ref[...], preferred_element_type=F32).astype(BF16)
    kr = proj[:, o + 2 * LORA:o + 2 * LORA + ROPE_DIM]
    k_rope = rope(jnp.concatenate([kr, jnp.zeros((tm, LANES - ROPE_DIM), F32)], axis=1)).astype(BF16)
    kr32 = k_rope.astype(F32)
    rope_sq = jnp.sum(kr32 * kr32, axis=1, keepdims=True)
    key_sq = []
    for hd in range(MLA_HEADS):
        c0 = hd * MLA_KV_STRIDE
        k_nope = kv[:, hd * 2 * HEAD_DIM:hd * 2 * HEAD_DIM + HEAD_DIM]
        mkv_ref[:, c0:c0 + HEAD_DIM] = k_nope
        mkv_ref[:, c0 + HEAD_DIM:c0 + 2 * HEAD_DIM] = k_rope
        mkv_ref[:, c0 + 2 * HEAD_DIM:c0 + 3 * HEAD_DIM] = kv[:, hd * 2 * HEAD_DIM + HEAD_DIM:(hd + 1) * 2 * HEAD_DIM]
        kn32 = k_nope.astype(F32)
        sq = jnp.sum(kn32 * kn32, axis=1, keepdims=True) + rope_sq
        key_sq.append(jnp.broadcast_to(jnp.max(sq, axis=0, keepdims=True), (1, LANES)))
    ksq_ref[0] = jnp.concatenate(key_sq, axis=0)


def _inproj(x2, sc1, sh1, g_attn, pos, rope_tables, w_all, g_q, wq, g_kv, wkv, seq):
    n, d = x2.shape
    tm = ROW_TILE
    wcols = w_all.shape[1]
    per_b = seq // tm
    row = lambda i: (i, 0)
    bat = lambda i: (i // per_b, 0, 0)
    outs = [
        jax.ShapeDtypeStruct((n, SB_WIDTH), BF16),
        jax.ShapeDtypeStruct((n, SB_WIDTH), BF16),
        jax.ShapeDtypeStruct((n, SB_WIDTH), BF16),
        jax.ShapeDtypeStruct((n, 2 * MLA_WIDTH), BF16),
        jax.ShapeDtypeStruct((n, MLA_HEADS * MLA_KV_STRIDE), BF16),
        jax.ShapeDtypeStruct((n // tm, MLA_HEADS, LANES), F32),
    ]
    return pl.pallas_call(
        _inproj_kernel,
        out_shape=outs,
        grid=(n // tm,),
        in_specs=[
            pl.BlockSpec((tm, d), row),
            pl.BlockSpec((1, 1, d), bat),
            pl.BlockSpec((1, 1, d), bat),
            _resident((1, d)),
            pl.BlockSpec((1, tm), lambda i: (0, i)),
            _resident((1, LANES)),
            _resident((1, LANES)),
            _resident((1, LANES)),
            _resident((1, LANES)),
            _resident((d, wcols)),
            _resident((1, LORA)),
            _resident((LORA, 2 * MLA_WIDTH)),
            _resident((1, LORA)),
            _resident((LORA, 2 * MLA_WIDTH)),
        ],
        out_specs=[
            pl.BlockSpec((tm, SB_WIDTH), row),
            pl.BlockSpec((tm, SB_WIDTH), row),
            pl.BlockSpec((tm, SB_WIDTH), row),
            pl.BlockSpec((tm, 2 * MLA_WIDTH), row),
            pl.BlockSpec((tm, MLA_HEADS * MLA_KV_STRIDE), row),
            pl.BlockSpec((1, MLA_HEADS, LANES), lambda i: (i, 0, 0)),
        ],
        compiler_params=_cparams(("arbitrary",)),
        name="inproj",
    )(x2, sc1, sh1, g_attn, pos, *rope_tables, w_all, g_q, wq, g_kv, wkv)


_NT = (((1,), (1,)), ((), ()))


def _sb_kernel(q_ref, k_ref, v_ref, o_ref, *, tile, heads):
    i = pl.program_id(2)
    row = lax.broadcasted_iota(I32, (tile, tile), 0)
    col = lax.broadcasted_iota(I32, (tile, tile), 1)
    suffix = jnp.where(row >= col, 1.0, 0.0).astype(BF16)
    strict = col < row

    def head_step(h, kt, c, acc, diag):
        hs = slice(h * HEAD_DIM, (h + 1) * HEAD_DIM)
        k0 = pl.multiple_of(kt * tile, tile)
        k = k_ref[pl.ds(k0, tile), hs]
        v = v_ref[pl.ds(k0, tile), hs]
        z = lax.dot_general(q_ref[:, hs], k, _NT, preferred_element_type=F32)
        log_beta = jnp.minimum(z, 0.0) - jnp.log(1.0 + jnp.exp(-jnp.abs(z)))
        log_keep = log_beta - z
        if diag:
            log_keep = jnp.where(strict, log_keep, 0.0)
        incl = jnp.dot(log_keep.astype(BF16), suffix, preferred_element_type=F32)
        a = jnp.exp(incl + z + c)
        if diag:
            a = jnp.where(strict, a, 0.0)
        acc = acc + jnp.dot(a.astype(BF16), v, preferred_element_type=F32)
        return c + incl[:, 0:1], acc

    def step(kt, carry, diag):
        return tuple(head_step(h, kt, *carry[h], diag) for h in range(heads))

    def live(carry):
        c_max = carry[0][0]
        for h in range(1, heads):
            c_max = jnp.maximum(c_max, carry[h][0])
        return (jnp.max(c_max) > SB_DEAD_LOG).astype(I32)

    zero = ((jnp.zeros((tile, 1), F32), jnp.zeros((tile, HEAD_DIM), F32)),) * heads

    def first_two():
        return step(i - 1, step(i, zero, True), False)

    carry = lax.cond(i > 0, first_two, lambda: step(i, zero, True))

    def body(state):
        kt, _, cr = state
        cr = step(kt, cr, False)
        return kt - 1, live(cr), cr

    _, _, carry = lax.while_loop(lambda st: (st[0] >= 0) & (st[1] > 0), body, (i - 2, live(carry), carry))
    for h in range(heads):
        o_ref[:, h * HEAD_DIM:(h + 1) * HEAD_DIM] = carry[h][1].astype(BF16)


def _sb_attention(sbq, sbk, sbv, batch, seq):
    n = sbq.shape[0]
    t = ATTN_TILE
    hb = ATTN_HEADS_PER_STEP
    nq = seq // t
    qmap = lambda b, h, i: (b * nq + i, h)
    kmap = lambda b, h, i: (b, h)
    return pl.pallas_call(
        functools.partial(_sb_kernel, tile=t, heads=hb),
        out_shape=jax.ShapeDtypeStruct((n, SB_WIDTH), BF16),
        grid=(batch, SB_HEADS // hb, nq),
        in_specs=[
            pl.BlockSpec((t, hb * HEAD_DIM), qmap),
            pl.BlockSpec((seq, hb * HEAD_DIM), kmap),
            pl.BlockSpec((seq, hb * HEAD_DIM), kmap),
        ],
        out_specs=pl.BlockSpec((t, hb * HEAD_DIM), qmap),
        compiler_params=_cparams(("arbitrary", "arbitrary", "arbitrary")),
        name="sb_attn",
    )(sbq, sbk, sbv)


def _mla_kernel(q_ref, kv_ref, kn_ref, o_ref, *, tq, tk, heads):
    i = pl.program_id(2)
    row = lax.broadcasted_iota(I32, (tq, tk), 0)
    col = lax.broadcasted_iota(I32, (tq, tk), 1)
    qk = 2 * HEAD_DIM
    per_q = tq // tk

    def scores(h, kt, diag):
        k0 = pl.multiple_of(kt * tk, tk)
        k = kv_ref[pl.ds(k0, tk), h * MLA_KV_STRIDE:h * MLA_KV_STRIDE + qk]
        s = lax.dot_general(q_ref[:, h * qk:(h + 1) * qk], k, _NT, preferred_element_type=F32)
        if diag:
            s = jnp.where(col + kt * tk <= row + i * tq, s, NEG_BIG)
        return s

    def lane_fold(x, op):
        out = x[:, :LANES]
        for cblk in range(1, tk // LANES):
            out = op(out, x[:, cblk * LANES:(cblk + 1) * LANES])
        return out

    def over_keys(step, carry):
        for r in range(per_q):
            carry = step(i * per_q + r, carry, True)
        return lax.fori_loop(0, i * per_q, lambda s, cr: step(s, cr, False), carry)

    def softmax_pv(shift):
        def acc_step(kt, carry, diag):
            out = []
            for h in range(heads):
                l, acc = carry[h]
                k0 = pl.multiple_of(kt * tk, tk)
                v = kv_ref[pl.ds(k0, tk), h * MLA_KV_STRIDE + qk:(h + 1) * MLA_KV_STRIDE]
                if diag and tq == tk:
                    hq = tq // 2
                    kb = h * MLA_KV_STRIDE
                    qh = q_ref[:, h * qk:(h + 1) * qk]
                    tri = (lax.broadcasted_iota(I32, (hq, hq), 1) <= lax.broadcasted_iota(I32, (hq, hq), 0))
                    s_a = lax.dot_general(qh, kv_ref[pl.ds(k0, hq), kb:kb + qk], _NT,
                                          preferred_element_type=F32)
                    s_a = jnp.concatenate([jnp.where(tri, s_a[:hq], NEG_BIG), s_a[hq:]], axis=0)
                    s_b = lax.dot_general(qh[hq:], kv_ref[pl.ds(k0 + hq, hq), kb:kb + qk], _NT,
                                          preferred_element_type=F32)
                    s_b = jnp.where(tri, s_b, NEG_BIG)
                    fold = lambda x: functools.reduce(
                        jnp.add, [x[:, c * LANES:(c + 1) * LANES] for c in range(hq // LANES)])
                    ex = lambda x, sh: jnp.concatenate(
                        [jnp.exp(x[:, c * LANES:(c + 1) * LANES] - sh) for c in range(hq // LANES)], axis=1)
                    p_a = ex(s_a, shift[h])
                    p_b = ex(s_b, shift[h][hq:])
                    pad_l = jnp.zeros((hq, LANES), F32)
                    pad_a = jnp.zeros((hq, HEAD_DIM), F32)
                    pv_b = jnp.dot(p_b.astype(BF16), v[hq:], preferred_element_type=F32)
                    out.append((l + fold(p_a) + jnp.concatenate([pad_l, fold(p_b)], axis=0),
                                acc + jnp.dot(p_a.astype(BF16), v[:hq], preferred_element_type=F32)
                                + jnp.concatenate([pad_a, pv_b], axis=0)))
                    continue
                s = scores(h, kt, diag)
                p = jnp.concatenate([jnp.exp(s[:, c * LANES:(c + 1) * LANES] - shift[h])
                                     for c in range(tk // LANES)], axis=1)
                out.append((l + lane_fold(p, jnp.add),
                            acc + jnp.dot(p.astype(BF16), v, preferred_element_type=F32)))
            return tuple(out)

        return over_keys(acc_step, ((jnp.zeros((tq, LANES), F32), jnp.zeros((tq, HEAD_DIM), F32)),) * heads)

    def row_sums(carry):
        return tuple(jnp.sum(carry[h][0], axis=1, keepdims=True) for h in range(heads))

    def write(carry, sums):
        for h in range(heads):
            o_ref[:, h * HEAD_DIM:(h + 1) * HEAD_DIM] = (carry[h][1] / sums[h]).astype(BF16)

    head0 = pl.program_id(1) * heads
    bound = []
    for h in range(heads):
        qh = q_ref[:, h * qk:(h + 1) * qk].astype(F32)
        q_norm = jnp.sqrt(jnp.sum(qh * qh, axis=1, keepdims=True))
        bound.append(q_norm * kn_ref[0, pl.ds(head0 + h, 1), :] * MLA_BOUND_SLACK)
    fast = softmax_pv(tuple(bound))
    fast_sums = row_sums(fast)
    smallest = fast_sums[0]
    for h in range(1, heads):
        smallest = jnp.minimum(smallest, fast_sums[h])
    in_range = jnp.min(smallest) > MLA_MIN_ROW_SUM
    write(fast, fast_sums)

    @pl.when(jnp.logical_not(in_range))
    def _():
        def max_step(kt, ms, diag):
            return tuple(jnp.maximum(ms[h], lane_fold(scores(h, kt, diag), jnp.maximum)) for h in range(heads))

        ms = over_keys(max_step, (jnp.full((tq, LANES), NEG_BIG, F32),) * heads)
        row_max = tuple(jnp.broadcast_to(jnp.max(ms[h], axis=1, keepdims=True), (tq, LANES))
                        for h in range(heads))
        exact = softmax_pv(row_max)
        write(exact, row_sums(exact))


def _mla_attention(mq, mkv, k_norm, batch, seq):
    n = mq.shape[0]
    t = MLA_Q_TILE
    hb = MLA_HEADS_PER_STEP
    nq = seq // t
    return pl.pallas_call(
        functools.partial(_mla_kernel, tq=t, tk=MLA_K_TILE, heads=hb),
        out_shape=jax.ShapeDtypeStruct((n, MLA_WIDTH), BF16),
        grid=(batch, MLA_HEADS // hb, nq),
        in_specs=[
            pl.BlockSpec((t, hb * 2 * HEAD_DIM), lambda b, h, i: (b * nq + i, h)),
            pl.BlockSpec((seq, hb * MLA_KV_STRIDE), lambda b, h, i: (b, h), pipeline_mode=pl.Buffered(1)),
            pl.BlockSpec((1, MLA_HEADS, LANES), lambda b, h, i: (b, 0, 0)),
        ],
        out_specs=pl.BlockSpec((t, hb * HEAD_DIM), lambda b, h, i: (b * nq + i, h)),
        compiler_params=_cparams(("arbitrary", "arbitrary", "arbitrary"), MLA_VMEM_LIMIT_BYTES),
        name="mla_attn",
    )(mq, mkv, k_norm)


def _outproj_kernel(sb_ref, mla_ref, x_ref, g1_ref, sc_ref, sh_ref, g_ref, wo_ref, wr_ref, br_ref,
                    x1_ref, hp_ref, ii_ref, iw_ref, cnt_ref, run_ref):
    step = pl.program_id(0)
    tm, d = x_ref.shape

    @pl.when(step == 0)
    def _():
        run_ref[...] = jnp.zeros_like(run_ref)

    o = (jnp.dot(sb_ref[...], wo_ref[:SB_WIDTH, :], preferred_element_type=F32)
         + jnp.dot(mla_ref[...], wo_ref[SB_WIDTH:, :], preferred_element_type=F32))
    x1 = x_ref[...] + g1_ref[0] * o
    x1_ref[...] = x1
    h2 = _rms(x1) * g_ref[...] * sc_ref[0] + sh_ref[0]
    hb = h2.astype(BF16)
    hb32 = hb.astype(F32)
    hp_ref[...] = hb32

    h_lo = (h2 - hb32).astype(BF16)
    w = wr_ref[...]
    w_hi = w.astype(BF16)
    w_lo = (w - w_hi.astype(F32)).astype(BF16)
    logits = (lax.dot_general(w_hi, hb, _NT, preferred_element_type=F32)
              + lax.dot_general(w_hi, h_lo, _NT, preferred_element_type=F32)
              + lax.dot_general(w_lo, hb, _NT, preferred_element_type=F32)) + br_ref[...]

    expert = lax.broadcasted_iota(I32, (N_EXPERTS, tm), 0)
    work = logits
    vals, idxs, hots = [], [], []
    for _ in range(TOP_K):
        m = jnp.max(work, axis=0, keepdims=True)
        idx = jnp.min(jnp.where(work == m, expert, N_EXPERTS), axis=0, keepdims=True)
        hot = expert == idx
        work = jnp.where(hot, -jnp.inf, work)
        vals.append(m)
        idxs.append(idx)
        hots.append(hot)
    exps = [jnp.exp(v - vals[0]) for v in vals]
    den = exps[0] + exps[1] + exps[2] + exps[3]
    sel = jnp.where(hots[0] | hots[1] | hots[2] | hots[3], 1.0, 0.0)

    r = lax.broadcasted_iota(I32, (tm, tm), 0)
    c = lax.broadcasted_iota(I32, (tm, tm), 1)
    earlier = jnp.where(r < c, 1.0, 0.0).astype(BF16)
    rank = jnp.dot(sel.astype(BF16), earlier, preferred_element_type=F32) + run_ref[...]
    run_ref[...] = run_ref[...] + jnp.sum(sel, axis=1, keepdims=True)
    cnt_ref[...] = run_ref[...]

    field = lax.broadcasted_iota(I32, (2 * TOP_K, tm), 0)
    info_i = jnp.zeros((2 * TOP_K, tm), I32)
    info_w = jnp.zeros((2 * TOP_K, tm), F32)
    for k in range(TOP_K):
        rank_k = jnp.sum(jnp.where(hots[k], rank, 0.0), axis=0, keepdims=True).astype(I32)
        info_i = jnp.where(field == k, idxs[k], info_i)
        info_i = jnp.where(field == TOP_K + k, rank_k, info_i)
        info_w = jnp.where(field == k, exps[k] / den, info_w)
    ii_ref[...] = info_i
    iw_ref[...] = info_w


def _outproj(sb, mla, x2, g1, sc2, sh2, g_ffn, wo, w_router, b_router, seq):
    n, d = x2.shape
    tm = OUTPROJ_TILE
    per_b = seq // tm
    row = lambda i: (i, 0)
    bat = lambda i: (i // per_b, 0, 0)
    outs = [
        jax.ShapeDtypeStruct((n, d), F32),
        jax.ShapeDtypeStruct((n, d), F32),
        jax.ShapeDtypeStruct((2 * TOP_K, n), I32),
        jax.ShapeDtypeStruct((2 * TOP_K, n), F32),
        jax.ShapeDtypeStruct((N_EXPERTS, 1), F32),
    ]
    return pl.pallas_call(
        _outproj_kernel,
        out_shape=outs,
        grid=(n // tm,),
        in_specs=[
            pl.BlockSpec((tm, SB_WIDTH), row),
            pl.BlockSpec((tm, MLA_WIDTH), row),
            pl.BlockSpec((tm, d), row),
            pl.BlockSpec((1, 1, d), bat),
            pl.BlockSpec((1, 1, d), bat),
            pl.BlockSpec((1, 1, d), bat),
            _resident((1, d)),
            _resident((SB_WIDTH + MLA_WIDTH, d)),
            _resident((N_EXPERTS, d)),
            _resident((N_EXPERTS, 1)),
        ],
        out_specs=[
            pl.BlockSpec((tm, d), row),
            pl.BlockSpec((tm, d), row),
            pl.BlockSpec((2 * TOP_K, tm), lambda i: (0, i)),
            pl.BlockSpec((2 * TOP_K, tm), lambda i: (0, i)),
            pl.BlockSpec((N_EXPERTS, 1), lambda i: (0, 0)),
        ],
        scratch_shapes=[pltpu.VMEM((N_EXPERTS, 1), F32)],
        compiler_params=_cparams(("arbitrary",)),
        name="outproj",
    )(sb, mla, x2, g1, sc2, sh2, g_ffn, wo, w_router, b_router)


def _row_copy(src_ref, src_row, dst_ref, dst_row, sem):
    return pltpu.make_async_copy(src_ref.at[pl.ds(src_row, 1), :], dst_ref.at[pl.ds(dst_row, 1), :], sem)


def _dispatch_kernel(pad_lo_ref, pad_hi_ref, dest_ref, h_ref, xs_ref, zero_ref, sem):
    tm = h_ref.shape[0]

    @pl.when(pl.program_id(0) == 0)
    def _():
        zero_ref[...] = jnp.zeros_like(zero_ref)

        def put(r, c):
            _row_copy(zero_ref, 0, xs_ref, r, sem).start()
            return c

        def got(r, c):
            _row_copy(zero_ref, 0, xs_ref, 0, sem).wait()
            return c

        def over_pad_rows(fn):
            def per_expert(e, carry):
                return lax.fori_loop(pad_lo_ref[e], pad_hi_ref[e], fn, carry)

            lax.fori_loop(0, N_EXPERTS, per_expert, 0)

        over_pad_rows(put)
        over_pad_rows(got)

    def issue(r, carry):
        for k in range(TOP_K):
            _row_copy(h_ref, r, xs_ref, dest_ref[r * TOP_K + k], sem).start(priority=k % 2)
        return carry

    lax.fori_loop(0, tm, issue, 0)
    for k in range(TOP_K):
        pltpu.make_async_copy(h_ref, xs_ref.at[pl.ds(0, tm), :], sem).wait()


def _dispatch(pad_lo, pad_hi, dest_flat, hp, rows):
    n, w = hp.shape
    tm = ROW_TILE
    grid_spec = pltpu.PrefetchScalarGridSpec(
        num_scalar_prefetch=2,
        grid=(n // tm,),
        in_specs=[
            pl.BlockSpec((tm * TOP_K,), lambda i, lo, hi: (i,), memory_space=pltpu.SMEM),
            pl.BlockSpec((tm, w), lambda i, lo, hi: (i, 0)),
        ],
        out_specs=pl.BlockSpec(memory_space=pl.ANY),
        scratch_shapes=[pltpu.VMEM((8, w), F32), pltpu.SemaphoreType.DMA(())],
    )
    return pl.pallas_call(
        _dispatch_kernel,
        out_shape=jax.ShapeDtypeStruct((rows, w), F32),
        grid_spec=grid_spec,
        compiler_params=_cparams(("arbitrary",)),
        name="dispatch",
    )(pad_lo, pad_hi, dest_flat, hp)


def _moe_kernel(ie_ref, ix_ref, ic_ref, iv_ref, ns_ref, xs_ref, w1_ref, b1g_ref, b1l_ref, w2_ref, b2_ref,
                p_ref, ys_ref, xb_ref, gl_ref, w2b_ref):
    del ie_ref, ix_ref
    item = pl.program_id(0)
    nj, fc, _ = w2b_ref.shape
    chunk = ic_ref[item]

    def ffn(j, rows):
        xt = xb_ref[:rows, :]
        ug = jnp.dot(xt, gl_ref[j, :, :fc], preferred_element_type=F32) + b1g_ref[j]
        ul = jnp.dot(xt, gl_ref[j, :, fc:], preferred_element_type=F32) + b1l_ref[j]
        glu = jnp.minimum(ug, SWIGLU_LIMIT)
        lin = jnp.clip(ul, -SWIGLU_LIMIT, SWIGLU_LIMIT)
        act = glu * jax.nn.sigmoid(SWIGLU_ALPHA * glu) * (lin + 1.0)
        ys_ref[:rows, :] += jnp.dot(act.astype(BF16), w2b_ref[j], preferred_element_type=F32)

    def for_used_rows(fn):
        for n_sub in range(1, MOE_TILE // MOE_SUB + 1):
            pl.when(ns_ref[item] == n_sub)(functools.partial(fn, n_sub * MOE_SUB))

    @pl.when(iv_ref[item] == 1)
    def _():
        @pl.when((chunk == 0) | (chunk == nj))
        def _():
            def load_rows(rows):
                xb_ref[:rows, :] = xs_ref[:rows, :].astype(BF16)

            for_used_rows(load_rows)
            ys_ref[...] = jnp.broadcast_to(b2_ref[...], ys_ref.shape)

        @pl.when(chunk < nj)
        def _():
            p = p_ref[...]
            for blk in range(2 * fc // 256):
                wb = w1_ref[:, blk * 256:(blk + 1) * 256].astype(BF16)
                split = jnp.dot(wb, p, preferred_element_type=F32).astype(BF16)
                gl_ref[chunk, :, blk * LANES:(blk + 1) * LANES] = split[:, :LANES]
                gl_ref[chunk, :, fc + blk * LANES:fc + (blk + 1) * LANES] = split[:, LANES:]
            w2b_ref[chunk] = w2_ref[...].astype(BF16)
            for_used_rows(functools.partial(ffn, chunk))

        @pl.when(chunk == nj)
        def _():
            def all_chunks(rows):
                def body(j, carry):
                    ffn(j, rows)
                    return carry

                lax.fori_loop(0, nj, body, 0, unroll=4)

            for_used_rows(all_chunks)


def _lookup(table, idx):
    hit = idx[..., None] == jnp.arange(table.shape[0], dtype=idx.dtype)
    return jnp.sum(jnp.where(hit, table, 0), axis=-1)


def _moe_items(tile_e, tile_v, tile_f, tile_ns, nj):
    n_tiles = tile_v.shape[0]
    n_items = n_tiles + N_EXPERTS * (nj - 1)
    cost = jnp.where(tile_f == 1, nj, 1) * tile_v
    first_item = jnp.cumsum(cost) - cost
    total = jnp.sum(cost)
    last_tile = jnp.maximum(jnp.sum(tile_v) - 1, 0)
    item = jnp.arange(n_items, dtype=I32)
    valid = item < total
    tile = jnp.where(valid, jnp.sum(first_item[None, :] <= item[:, None], axis=1) - 1, last_tile).astype(I32)
    chunk = jnp.where(valid & (_lookup(tile_f, tile) == 1), item - _lookup(first_item, tile), nj).astype(I32)
    return _lookup(tile_e, tile), tile, chunk, valid.astype(I32), (_lookup(tile_ns, tile) * valid).astype(I32)


def _moe(item_e, item_x, item_c, item_v, item_ns, xs, w1, b1g, b1l, w2, b2, perm):
    rows, d = xs.shape
    f = w1.shape[2] // 2
    fc = MOE_FC
    nj = f // fc

    def wj(q, ic):
        return jnp.minimum(ic[q], nj - 1)

    grid_spec = pltpu.PrefetchScalarGridSpec(
        num_scalar_prefetch=5,
        grid=(item_e.shape[0],),
        in_specs=[
            pl.BlockSpec((MOE_TILE, d), lambda q, ie, ix, ic, iv, ns: (ix[q], 0)),
            pl.BlockSpec((None, d, 2 * fc), lambda q, ie, ix, ic, iv, ns: (ie[q], 0, wj(q, ic))),
            pl.BlockSpec((None, nj, 1, fc), lambda q, ie, ix, ic, iv, ns: (ie[q], 0, 0, 0)),
            pl.BlockSpec((None, nj, 1, fc), lambda q, ie, ix, ic, iv, ns: (ie[q], 0, 0, 0)),
            pl.BlockSpec((None, fc, d), lambda q, ie, ix, ic, iv, ns: (ie[q], wj(q, ic), 0)),
            pl.BlockSpec((None, 1, d), lambda q, ie, ix, ic, iv, ns: (ie[q], 0, 0)),
            pl.BlockSpec((256, 256), lambda q, ie, ix, ic, iv, ns: (0, 0)),
        ],
        out_specs=pl.BlockSpec((MOE_TILE, d), lambda q, ie, ix, ic, iv, ns: (ix[q], 0)),
        scratch_shapes=[
            pltpu.VMEM((MOE_TILE, d), BF16),
            pltpu.VMEM((nj, d, 2 * fc), BF16),
            pltpu.VMEM((nj, fc, d), BF16),
        ],
    )
    return pl.pallas_call(
        _moe_kernel,
        out_shape=jax.ShapeDtypeStruct((rows, d), F32),
        grid_spec=grid_spec,
        compiler_params=_cparams(("arbitrary",)),
        name="moe",
    )(item_e, item_x, item_c, item_v, item_ns, xs, w1, b1g, b1l, w2, b2, perm)


def _combine_kernel(dest_ref, next_ref, w_ref, x1_ref, g2_ref, gf_ref, ys_ref, o_ref, buf_ref, sem):
    i = pl.program_id(0)
    tm = x1_ref.shape[0]

    def gather(idx_ref, s):
        def issue(r, carry):
            for k in range(TOP_K):
                _row_copy(ys_ref, idx_ref[r * TOP_K + k], buf_ref.at[s], k * tm + r, sem.at[s]).start(priority=k % 2)
            return carry

        lax.fori_loop(0, tm, issue, 0)

    @pl.when(i == 0)
    def _():
        gather(dest_ref, 0)

    def reduce_slot(s):
        @pl.when(i + 1 < pl.num_programs(0))
        def _():
            gather(next_ref, 1 - s)

        pltpu.make_async_copy(ys_ref.at[pl.ds(0, TOP_K * tm), :], buf_ref.at[s], sem.at[s]).wait()
        w = w_ref[...]
        moe = w[:, 0:1] * buf_ref[s, :tm, :]
        for k in range(1, TOP_K):
            moe = moe + w[:, k:k + 1] * buf_ref[s, k * tm:(k + 1) * tm, :]
        x = x1_ref[...] + g2_ref[0] * moe
        o_ref[...] = _rms(x) * gf_ref[...]

    for s in range(2):
        pl.when(i % 2 == s)(functools.partial(reduce_slot, s))


def _combine(dest_flat, weights, x1, g2, g_final, ys, seq):
    n, d = x1.shape
    tm = COMBINE_TILE
    per_b = seq // tm
    steps = n // tm
    return pl.pallas_call(
        _combine_kernel,
        out_shape=jax.ShapeDtypeStruct((n, d), F32),
        grid=(steps,),
        in_specs=[
            pl.BlockSpec((tm * TOP_K,), lambda i: (i,), memory_space=pltpu.SMEM),
            pl.BlockSpec((tm * TOP_K,), lambda i: (jnp.minimum(i + 1, steps - 1),), memory_space=pltpu.SMEM),
            pl.BlockSpec((tm, TOP_K), lambda i: (i, 0)),
            pl.BlockSpec((tm, d), lambda i: (i, 0)),
            pl.BlockSpec((1, 1, d), lambda i: (i // per_b, 0, 0)),
            _resident((1, d)),
            pl.BlockSpec(memory_space=pl.ANY),
        ],
        out_specs=pl.BlockSpec((tm, d), lambda i: (i, 0)),
        scratch_shapes=[pltpu.VMEM((2, TOP_K * tm, d), F32), pltpu.SemaphoreType.DMA((2,))],
        compiler_params=_cparams(("arbitrary",)),
        name="combine",
    )(dest_flat, dest_flat, weights, x1, g2, g_final, ys)


def _rope_lane_tables():
    inv_freq = 1.0 / (ROPE_THETA ** (jnp.arange(ROPE_HALF, dtype=F32) * 2.0 / ROPE_DIM))
    z = jnp.zeros((ROPE_HALF,), F32)
    o = jnp.ones((ROPE_HALF,), F32)
    invf = jnp.concatenate([inv_freq, inv_freq, z, z])[None, :]
    cmask = jnp.concatenate([o, o, z, z])[None, :]
    smask_x2 = jnp.concatenate([-o, z, z, z])[None, :]
    smask_x1 = jnp.concatenate([z, o, z, z])[None, :]
    return invf, cmask, smask_x2, smask_x1


def _pad_rope_cols(w):
    return jnp.concatenate([w, jnp.zeros(w.shape[:-1] + (LANES - ROPE_DIM,), w.dtype)], axis=-1)


def kernel(x, c, positions, g_attn, w_mod, b_mod, w_in, g_q_lat, w_q_up, g_kv_lat, w_kv_up, w_out, g_ffn,
           w_router, b_router, w1, b1, w2, b2, g_final):
    batch, seq, d = x.shape
    n = batch * seq
    depth = w_mod.shape[0]
    assert depth == 1 and seq % ATTN_TILE == 0 and seq % ROW_TILE == 0 and d == SB_WIDTH + MLA_WIDTH
    x2 = x.reshape(n, d)

    c_pad = jnp.zeros((8, d), F32).at[:batch].set(c)
    mod = _mod(c_pad, w_mod[0], b_mod[0][None, :])[:batch]
    shift1, scale1, gate1, shift2, scale2, gate2 = [m[:, None, :] for m in jnp.split(mod, 6, axis=-1)]

    w_all = w_in[0].astype(BF16)
    wq = w_q_up[0].reshape(LORA, MLA_HEADS, MLA_QK_DIM)
    wq = jnp.concatenate([wq[..., :HEAD_DIM], _pad_rope_cols(wq[..., HEAD_DIM:])], axis=-1)
    wq = wq.reshape(LORA, MLA_HEADS * 2 * HEAD_DIM).astype(BF16)
    wkv = w_kv_up[0].astype(BF16)

    sbq, sbk, sbv, mq, mkv, key_sq = _inproj(
        x2, 1.0 + scale1, shift1, g_attn, positions.reshape(1, n), _rope_lane_tables(), w_all,
        g_q_lat, wq, g_kv_lat, wkv, seq)

    sb_out = _sb_attention(sbq, sbk, sbv, batch, seq)
    k_norm = jnp.sqrt(jnp.max(key_sq.reshape(batch, -1, MLA_HEADS, LANES), axis=1))
    mla_out = _mla_attention(mq, mkv, k_norm, batch, seq)

    x1, hp, info_i, info_w, counts = _outproj(
        sb_out, mla_out, x2, gate1, 1.0 + scale2, shift2, g_ffn, w_out[0].astype(BF16),
        w_router[0].T, b_router.reshape(N_EXPERTS, 1), seq)

    counts = counts[:, 0].astype(I32)
    padded = (counts + MOE_TILE - 1) // MOE_TILE * MOE_TILE
    pend = jnp.cumsum(padded)
    pstart = pend - padded
    idx = info_i[:TOP_K].T
    dest = (_lookup(pstart, idx) + info_i[TOP_K:].T).reshape(n * TOP_K)
    rows = n * TOP_K + N_EXPERTS * MOE_TILE
    n_tiles = rows // MOE_TILE
    tile_row0 = jnp.arange(n_tiles, dtype=I32) * MOE_TILE
    tile_v = (tile_row0 < pend[-1]).astype(I32)
    last = jnp.maximum(pend[-1] // MOE_TILE - 1, 0)
    tile_x = jnp.minimum(jnp.arange(n_tiles, dtype=I32), last)
    tile_e = jnp.minimum(jnp.sum(pend[None, :] <= (tile_x * MOE_TILE)[:, None], axis=1), N_EXPERTS - 1).astype(I32)
    tile_start = _lookup(pstart, tile_e)
    tile_f = (tile_x * MOE_TILE == tile_start).astype(I32) * tile_v
    used = _lookup(counts, tile_e) - (tile_x * MOE_TILE - tile_start)
    tile_ns = jnp.clip((used + MOE_SUB - 1) // MOE_SUB, 0, MOE_TILE // MOE_SUB) * tile_v

    xs = _dispatch(pstart + counts, pstart + (counts + MOE_SUB - 1) // MOE_SUB * MOE_SUB, dest, hp, rows)

    src = jnp.arange(256)
    perm = (jnp.arange(256)[None, :] == (src % 2 * LANES + src // 2)[:, None]).astype(BF16)
    f = w1.shape[-1] // 2
    nj = f // MOE_FC
    b1r = b1[0].reshape(N_EXPERTS, nj, 1, MOE_FC, 2)
    items = _moe_items(tile_e, tile_v, tile_f, tile_ns, nj)
    ys = _moe(*items, xs, w1[0], b1r[..., 0], b1r[..., 1], w2[0], b2[0][:, None, :], perm)

    out = _combine(dest, info_w[:TOP_K].T, x1, gate2, g_final[None, :], ys, seq)
    return out.reshape(batch, seq, d)
```
